```python
import jax, jax.numpy as jnp
from jax import lax
import numpy as np

D_MODEL = 1024
BATCH = 8
SEQ = 2048
DEPTH = 1
DEC_BATCH = 128
DEC_SEQ = 1
PAST_LEN = 8192
PAGE_SIZE = 128

N_META = 16
D_MIX = D_MODEL
HG_WIDTH = D_MIX // 2
HG_KDIM = 128
HG_HEADS = HG_WIDTH // HG_KDIM
HG_VDIM = HG_WIDTH // HG_HEADS
HG_QK = HG_HEADS * HG_KDIM
HG_CHUNK = 64
SB_WIDTH = D_MIX - HG_WIDTH
SB_DH = 64
SB_HEADS = SB_WIDTH // SB_DH
Q_BLOCK = 128
SB_BIAS_INIT = -6.0
D_PROJ = 2 * HG_QK + 2 * HG_WIDTH + 4 * SB_WIDTH
EPS = 1e-6

kernel_name = 'hymba_hgrn2_stickbreaking_step'


def _rms_f32(x, g):
    xf = x.astype(jnp.float32)
    return xf * lax.rsqrt(jnp.mean(xf * xf, axis=-1, keepdims=True) + EPS) * g.astype(jnp.float32)


def _lower_bound(lb_logits, layer):
    p = jax.nn.softmax(lb_logits.astype(jnp.float32), axis=0)
    return jnp.cumsum(p, axis=0)[layer]


def _project(x, norm_g, w_in, lb, q_norm_g, k_norm_g):
    B, T, _ = x.shape
    h = _rms_f32(x, norm_g).astype(x.dtype)
    u = jnp.einsum('btd,de->bte', h, w_in).astype(jnp.float32)
    o1 = HG_QK
    o2 = o1 + HG_QK
    o3 = o2 + HG_WIDTH
    o4 = o3 + HG_WIDTH
    o5 = o4 + SB_WIDTH
    o6 = o5 + SB_WIDTH
    o7 = o6 + SB_WIDTH
    hq, hf, hi, hgate, sq, sk, sv, sgate = jnp.split(u, [o1, o2, o3, o4, o5, o6, o7], axis=-1)
    hg_q = jax.nn.silu(hq).reshape(B, T, HG_HEADS, HG_KDIM)
    log_f = jnp.logaddexp(jnp.log(lb), jnp.log1p(-lb) + jax.nn.log_sigmoid(hf))
    log_f = log_f.reshape(B, T, HG_HEADS, HG_KDIM)
    hg_v = hi.reshape(B, T, HG_HEADS, HG_VDIM)
    sb_q = _rms_f32(sq.reshape(B, T, SB_HEADS, SB_DH), q_norm_g)
    sb_k = _rms_f32(sk.reshape(B, T, SB_HEADS, SB_DH), k_norm_g)
    sb_v = sv.reshape(B, T, SB_HEADS, SB_DH)
    return hg_q, log_f, hg_v, hgate, sb_q, sb_k, sb_v, sgate


def _merge(x, o_hg, gate_hg, o_sb, gate_sb, hg_norm_g, w_out):
    B, T, _ = x.shape
    a = _rms_f32(o_hg, hg_norm_g).reshape(B, T, HG_WIDTH) * jax.nn.silu(gate_hg)
    b = o_sb.reshape(B, T, SB_WIDTH) * jax.nn.silu(gate_sb)
    mix = jnp.concatenate([a, b], axis=-1).astype(x.dtype)
    return x + jnp.einsum('bte,ed->btd', mix, w_out).astype(x.dtype)


def _hgrn_chunk(S0, q, log_f, v):
    L = q.shape[1]
    k = -jnp.expm1(log_f)
    b = jnp.cumsum(log_f, axis=1)
    inter = jnp.einsum('blhk,bhkv->blhv', q * jnp.exp(b), S0)
    causal = jnp.tril(jnp.ones((L, L), dtype=bool))
    diff = b[:, :, None] - b[:, None, :]
    decay = jnp.where(causal[None, :, :, None, None], jnp.exp(jnp.minimum(diff, 0.0)), 0.0)
    A = jnp.einsum('bthk,btshk,bshk->bhts', q, decay, k)
    intra = jnp.einsum('bhts,bshv->bthv', A, v)
    b_last = b[:, -1]
    S_new = jnp.exp(b_last)[..., None] * S0 + jnp.einsum(
        'bshk,bshv->bhkv', k * jnp.exp(b_last[:, None] - b), v)
    return S_new, inter + intra


def _hgrn_prompt(q, log_f, v):
    B, T = q.shape[:2]
    pad = HG_CHUNK - N_META
    padw = ((0, 0), (pad, 0), (0, 0), (0, 0))
    q = jnp.pad(q, padw)
    log_f = jnp.pad(log_f, padw)
    v = jnp.pad(v, padw)
    n = (T + pad) // HG_CHUNK

    def to_chunks(a):
        return a.reshape(B, n, HG_CHUNK, *a.shape[2:]).swapaxes(0, 1)

    def step(S, c):
        return _hgrn_chunk(S, c[0], c[1], c[2])

    S0 = jnp.zeros((B, HG_HEADS, HG_KDIM, HG_VDIM), jnp.float32)
    S, o = lax.scan(step, S0, (to_chunks(q), to_chunks(log_f), to_chunks(v)))
    o = o.swapaxes(0, 1).reshape(B, T + pad, HG_HEADS, HG_VDIM)[:, pad:]
    return o, S


def _stick_breaking(q, k, v, bias, q_pos, k_pos):
    z = jnp.einsum('bqhd,bkhd->bhqk', q.astype(jnp.float32), k.astype(jnp.float32)) * (SB_DH ** -0.5)
    z = z + bias.astype(jnp.float32)[None, :, None, None]
    mask = k_pos[None, :] < q_pos[:, None]
    log_beta = jax.nn.log_sigmoid(z)
    log_keep = jnp.where(mask, jax.nn.log_sigmoid(-z), 0.0)
    later = lax.cumsum(log_keep, axis=3, reverse=True) - log_keep
    a = jnp.where(mask, jnp.exp(log_beta + later), 0.0)
    return jnp.einsum('bhqk,bkhd->bqhd', a, v.astype(jnp.float32))


def _sb_prompt(q, k, v, bias):
    B, T = q.shape[:2]
    pos = jnp.arange(T)
    o_meta = _stick_breaking(q[:, :N_META], k[:, :N_META], v[:, :N_META], bias, pos[:N_META], pos[:N_META])
    n = (T - N_META) // Q_BLOCK
    qb = q[:, N_META:].reshape(B, n, Q_BLOCK, SB_HEADS, SB_DH).swapaxes(0, 1)
    pb = pos[N_META:].reshape(n, Q_BLOCK)
    ob = lax.map(lambda c: _stick_breaking(c[0], k, v, bias, c[1], pos), (qb, pb))
    o_real = ob.swapaxes(0, 1).reshape(B, T - N_META, SB_HEADS, SB_DH)
    return jnp.concatenate([o_meta, o_real], axis=1)


def setup_inputs(seed: int = 0) -> dict:
    key = jax.random.key(seed)
    ks = jax.random.split(key, 15)
    n_pages = PAST_LEN // PAGE_SIZE
    n_phys = (DEC_BATCH * n_pages * 5) // 4
    page_table = jax.random.permutation(ks[0], n_phys)[:DEC_BATCH * n_pages]
    page_table = page_table.reshape(DEC_BATCH, n_pages).astype(jnp.int32)
    f32 = jnp.float32
    return {
        'x_prompt': jax.random.normal(ks[1], (BATCH, SEQ, D_MODEL), f32),
        'x_sample': jax.random.normal(ks[2], (DEC_BATCH, DEC_SEQ, D_MODEL), f32),
        'cache_k': jax.random.normal(ks[3], (DEPTH, n_phys, PAGE_SIZE, SB_HEADS, SB_DH), f32),
        'cache_v': jax.random.normal(ks[4], (DEPTH, n_phys, PAGE_SIZE, SB_HEADS, SB_DH), f32),
        'state_hgrn': 0.5 * jax.random.normal(ks[5], (DEPTH, DEC_BATCH, HG_HEADS, HG_KDIM, HG_VDIM), f32),
        'page_table': page_table,
        'meta_tokens': jax.random.normal(ks[6], (N_META, D_MODEL), f32),
        'norm_g': 1.0 + 0.02 * jax.random.normal(ks[7], (DEPTH, D_MODEL), f32),
        'w_in': jax.random.normal(ks[8], (DEPTH, D_MODEL, D_PROJ), f32) * D_MODEL ** -0.5,
        'lb_logits': 0.5 * jax.random.normal(ks[9], (DEPTH + 1, HG_QK), f32),
        'hg_norm_g': 1.0 + 0.02 * jax.random.normal(ks[10], (DEPTH, HG_HEADS, HG_VDIM), f32),
        'q_norm_g': 1.0 + 0.02 * jax.random.normal(ks[11], (DEPTH, SB_DH), f32),
        'k_norm_g': 1.0 + 0.02 * jax.random.normal(ks[12], (DEPTH, SB_DH), f32),
        'sb_bias': SB_BIAS_INIT + 0.3 * jax.random.normal(ks[14], (DEPTH, SB_HEADS), f32),
        'w_out': jax.random.normal(ks[13], (DEPTH, D_MIX, D_MODEL), f32) * D_MIX ** -0.5,
    }


def reference(x_prompt, x_sample, cache_k, cache_v, state_hgrn, page_table, meta_tokens, norm_g,
              w_in, lb_logits, hg_norm_g, q_norm_g, k_norm_g, sb_bias, w_out):
    B = x_prompt.shape[0]
    DB, DS = x_sample.shape[:2]
    past = page_table.shape[1] * PAGE_SIZE
    meta = jnp.broadcast_to(meta_tokens.astype(x_prompt.dtype)[None], (B, N_META, D_MODEL))
    xp = jnp.concatenate([meta, x_prompt], axis=1)
    xs = x_sample
    q_pos_s = past + jnp.arange(DS)
    k_pos_s = jnp.arange(past + DS)
    kp, vp, sp, ks_, vs, ss = [], [], [], [], [], []
    for layer in range(DEPTH):
        lb = _lower_bound(lb_logits, layer)
        hq, lf, hv, hgate, sq, sk, sv, sgate = _project(
            xp, norm_g[layer], w_in[layer], lb, q_norm_g[layer], k_norm_g[layer])
        o_hg, S_p = _hgrn_prompt(hq, lf, hv)
        o_sb = _sb_prompt(sq, sk, sv, sb_bias[layer])
        xp = _merge(xp, o_hg, hgate, o_sb, sgate, hg_norm_g[layer], w_out[layer])
        kp.append(sk.astype(cache_k.dtype))
        vp.append(sv.astype(cache_v.dtype))
        sp.append(S_p.astype(state_hgrn.dtype))
        hq, lf, hv, hgate, sq, sk, sv, sgate = _project(
            xs, norm_g[layer], w_in[layer], lb, q_norm_g[layer], k_norm_g[layer])
        S_s, o_hg = _hgrn_chunk(state_hgrn[layer].astype(jnp.float32), hq, lf, hv)
        past_k = cache_k[layer][page_table].reshape(DB, past, SB_HEADS, SB_DH).astype(jnp.float32)
        past_v = cache_v[layer][page_table].reshape(DB, past, SB_HEADS, SB_DH).astype(jnp.float32)
        k_all = jnp.concatenate([past_k, sk], axis=1)
        v_all = jnp.concatenate([past_v, sv], axis=1)
        o_sb = _stick_breaking(sq, k_all, v_all, sb_bias[layer], q_pos_s, k_pos_s)
        xs = _merge(xs, o_hg, hgate, o_sb, sgate, hg_norm_g[layer], w_out[layer])
        ks_.append(sk.astype(cache_k.dtype))
        vs.append(sv.astype(cache_v.dtype))
        ss.append(S_s.astype(state_hgrn.dtype))
    y_prompt = xp[:, N_META:]
    return (y_prompt, xs, jnp.stack(kp), jnp.stack(vp), jnp.stack(sp),
            jnp.stack(ks_), jnp.stack(vs), jnp.stack(ss))
```

```python
import functools

import jax
import jax.numpy as jnp
from jax import lax
from jax.experimental import pallas as pl
from jax.experimental.pallas import tpu as pltpu

F32 = jnp.float32
BF16 = jnp.bfloat16
EPS = 1e-6

SEG = 512
N_SEG = 8
HG_HEADS = 4
HG_DIM = 128
SB_HEADS = 8
SB_DH = 64
PAGE = 128
N_META = 16

HG_CHUNK = 128
SB_BLK = 256
PROJ_ROWS = 256
DEC_PAGES = 16
STEP_SAMPLES = 8

VMEM_LIMIT = 48 * 1024 * 1024

NT_DIMS = (((1,), (1,)), ((), ()))


def _dot(a, b):
    return jnp.dot(a, b, preferred_element_type=F32)


def _dot_nt(a, b):
    return lax.dot_general(a, b, NT_DIMS, preferred_element_type=F32)


def _split2(x):
    hi = x.astype(BF16)
    lo = (x - hi.astype(F32)).astype(BF16)
    return hi, lo


def _split3(x):
    hi = x.astype(BF16)
    r = x - hi.astype(F32)
    mid = r.astype(BF16)
    lo = (r - mid.astype(F32)).astype(BF16)
    return hi, mid, lo


def _softplus(z):
    return jnp.maximum(z, 0.0) + jnp.log1p(jnp.exp(-jnp.abs(z)))


def _silu(z):
    return z * (1.0 / (1.0 + jnp.exp(-z)))


def _proj_kernel(x_ref, g_ref, w_ref, lbl_ref, qg_ref, kg_ref, grp_ref,
                 hq_ref, lf_ref, hv_ref, hgate_ref, sq_ref, sk_ref, sv_ref, sgate_ref):
    x = x_ref[...]
    ms = jnp.mean(x * x, axis=-1, keepdims=True)
    h = (x * lax.rsqrt(ms + EPS) * g_ref[...]).astype(BF16)

    def seg(j):
        return _dot(h, w_ref[:, j * SEG:(j + 1) * SEG])

    def group_rms(u, gain):
        hi, lo = _split2(u * u)
        ss = _dot(hi, grp_ref[...]) + _dot(lo, grp_ref[...])
        return u * lax.rsqrt(ss * (1.0 / SB_DH) + EPS) * gain

    hq_ref[...] = _silu(seg(0))

    l = lbl_ref[...]
    e = jnp.exp(l - jnp.max(l, axis=0, keepdims=True))
    lb = e[0:1, :] / jnp.sum(e, axis=0, keepdims=True)
    a = jnp.log(lb)
    z = seg(1)
    bb = jnp.log1p(-lb) + (jnp.minimum(z, 0.0) - jnp.log1p(jnp.exp(-jnp.abs(z))))
    lf_ref[...] = jnp.maximum(a, bb) + jnp.log1p(jnp.exp(-jnp.abs(a - bb)))

    hv_ref[...] = seg(2)
    hgate_ref[...] = _silu(seg(3))
    sq_ref[...] = group_rms(seg(4), qg_ref[...])
    sk_ref[...] = group_rms(seg(5), kg_ref[...])
    sv_ref[...] = seg(6)
    sgate_ref[...] = _silu(seg(7))


def _project(x2d, tm, norm_g, w_bf16, lb_logits, qg, kg, grp):
    rows, d = x2d.shape
    const = lambda i: (0, 0)
    row_spec = pl.BlockSpec((tm, SEG), lambda i: (i, 0))
    return pl.pallas_call(
        _proj_kernel,
        grid=(rows // tm,),
        in_specs=[
            pl.BlockSpec((tm, d), lambda i: (i, 0)),
            pl.BlockSpec((1, d), const),
            pl.BlockSpec(w_bf16.shape, const),
            pl.BlockSpec(lb_logits.shape, const),
            pl.BlockSpec((1, SEG), const),
            pl.BlockSpec((1, SEG), const),
            pl.BlockSpec((SEG, SEG), const),
        ],
        out_specs=[row_spec] * N_SEG,
        out_shape=[jax.ShapeDtypeStruct((rows, SEG), F32)] * N_SEG,
        compiler_params=pltpu.CompilerParams(
            dimension_semantics=("parallel",), vmem_limit_bytes=VMEM_LIMIT),
        name="proj",
    )(x2d, norm_g, w_bf16, lb_logits, qg, kg, grp)


def _hgrn_chunk(q, lf, v, st, want_out):
    c = q.shape[0]
    width = q.shape[1]
    r = lax.broadcasted_iota(jnp.int32, (c, c), 0)
    s = lax.broadcasted_iota(jnp.int32, (c, c), 1)
    tri = (s <= r).astype(BF16)
    b = sum(_dot(tri, p) for p in _split3(lf))
    kk = 1.0 - jnp.exp(lf)
    b_last = b[c - 1:c, :]
    kdec = (kk * jnp.exp(jnp.minimum(b_last - b, 0.0))).astype(BF16)
    sdec = jnp.exp(b_last)

    new_st = []
    for h in range(HG_HEADS):
        sl = slice(h * HG_DIM, (h + 1) * HG_DIM)
        vt = v[:, sl].T.astype(BF16)
        new_st.append(st[h] * sdec[:, sl] + _dot(vt, kdec[:, sl]))
    if not want_out:
        return new_st, None

    levels = [(0, q, kk)]
    m = c // 2
    while m >= 8:
        nb = c // m
        starts = [b[i * m:i * m + 1, :] for i in range(nb)]
        hold = jnp.concatenate([jnp.broadcast_to(starts[i], (m, width)) for i in range(nb)], axis=0)
        nxt = jnp.concatenate(
            [jnp.broadcast_to(starts[min(i + 1, nb - 1)], (m, width)) for i in range(nb)], axis=0)
        levels.append((m, q * jnp.exp(jnp.minimum(b - hold, 0.0)),
                       kk * jnp.exp(jnp.minimum(nxt - b, 0.0))))
        m //= 2
    ri = lax.broadcasted_iota(jnp.int32, (c, width), 0)
    r4 = ri & 3
    r2 = ri & 1
    dn1 = pltpu.roll(lf, 1, 0)
    dn2 = pltpu.roll(lf, 2, 0)
    up1 = pltpu.roll(lf, c - 1, 0)
    up2 = pltpu.roll(lf, c - 2, 0)
    up3 = pltpu.roll(lf, c - 3, 0)
    up4 = pltpu.roll(lf, c - 4, 0)
    zero = jnp.zeros_like(lf)
    eq4 = jnp.where(r4 > 0, lf, zero) + jnp.where(r4 > 1, dn1, zero) + jnp.where(r4 > 2, dn2, zero)
    ek4 = up1 + jnp.where(r4 <= 2, up2, zero) + jnp.where(r4 <= 1, up3, zero) + jnp.where(r4 == 0, up4, zero)
    eq2 = jnp.where(r2 > 0, lf, zero)
    ek2 = up1 + jnp.where(r2 == 0, up2, zero)
    levels.append((4, q * jnp.exp(eq4), kk * jnp.exp(ek4)))
    levels.append((2, q * jnp.exp(eq2), kk * jnp.exp(ek2)))
    levels.append((1, q, kk * jnp.exp(up1)))

    x = r ^ s
    lower = s < r
    masks = [r == s if m == 0 else lower & (x >= m) & (x < 2 * m) for m, _, _ in levels]
    qdec = (q * jnp.exp(b)).astype(BF16)
    levels = [(qm.astype(BF16), km.astype(BF16)) for _, qm, km in levels]

    outs = []
    for h in range(HG_HEADS):
        sl = slice(h * HG_DIM, (h + 1) * HG_DIM)
        a = jnp.zeros((c, c), F32)
        for mask, (qm, km) in zip(masks, levels):
            a = jnp.where(mask, _dot_nt(qm[:, sl], km[:, sl]), a)
        o = _dot_nt(qdec[:, sl], st[h].astype(BF16)) + _dot(a.astype(BF16), v[:, sl].astype(BF16))
        outs.append(o)
    return new_st, outs


def _hgrn_meta_kernel(lf_ref, v_ref, st_ref):
    st0 = [jnp.zeros((HG_DIM, HG_DIM), F32)] * HG_HEADS
    lf = lf_ref[...]
    new_st, _ = _hgrn_chunk(lf, lf, v_ref[...], st0, False)
    for h in range(HG_HEADS):
        st_ref[h] = new_st[h]


def _hgrn_prompt_kernel(q_ref, lf_ref, v_ref, gate_ref, ng_ref, st0_ref, o_ref, s_ref, st_ref):
    ci = pl.program_id(1)

    @pl.when(ci == 0)
    def _():
        st_ref[...] = st0_ref[...]

    st = [st_ref[h] for h in range(HG_HEADS)]
    new_st, outs = _hgrn_chunk(q_ref[0], lf_ref[0], v_ref[0], st, True)
    for h in range(HG_HEADS):
        sl = slice(h * HG_DIM, (h + 1) * HG_DIM)
        st_ref[h] = new_st[h]
        o = outs[h]
        ms = jnp.mean(o * o, axis=-1, keepdims=True)
        o_ref[0, :, sl] = o * lax.rsqrt(ms + EPS) * ng_ref[:, sl] * gate_ref[0, :, sl]

    @pl.when(ci == pl.num_programs(1) - 1)
    def _():
        for h in range(HG_HEADS):
            s_ref[0, h] = new_st[h].T


def _hgrn_step_kernel(q_ref, lf_ref, v_ref, gate_ref, ng_ref, s0_ref, s_ref, a_ref):
    for i in range(STEP_SAMPLES):
        q = q_ref[i:i + 1, :]
        f = jnp.exp(lf_ref[i:i + 1, :])
        v = v_ref[i:i + 1, :]
        qk = q * (1.0 - f)
        for h in range(HG_HEADS):
            sl = slice(h * HG_DIM, (h + 1) * HG_DIM)
            fc = jnp.broadcast_to(f[:, sl], (HG_DIM, HG_DIM)).T
            sn = fc * s0_ref[i, h] + (1.0 - fc) * v[:, sl]
            s_ref[i, h] = sn
            qf = jnp.broadcast_to(q[:, sl] * f[:, sl], (8, HG_DIM)).astype(BF16)
            o = _dot(qf, s0_ref[i, h].astype(BF16))[0:1, :]
            o = o + jnp.sum(qk[:, sl], axis=-1, keepdims=True) * v[:, sl]
            ms = jnp.mean(o * o, axis=-1, keepdims=True)
            a_ref[i:i + 1, sl] = o * lax.rsqrt(ms + EPS) * ng_ref[:, sl] * gate_ref[i:i + 1, sl]


def _sb_block(qh, kb, vb, bias, carry, acc, mask, ustrict):
    z = _dot_nt(qh, kb.astype(BF16)) * (SB_DH ** -0.5) + bias
    sp = _softplus(z)
    lk = -sp
    if mask is not None:
        lk = jnp.where(mask, lk, 0.0)
    hi, lo = _split2(lk)
    cum = _dot(hi, ustrict) + _dot(lo, ustrict)
    a = jnp.exp((z - sp) + cum + carry)
    if mask is not None:
        a = jnp.where(mask, a, 0.0)
    acc = acc + _dot(a.astype(BF16), vb)
    carry = carry + jnp.sum(lk, axis=-1, keepdims=True)
    return carry, acc


def _sb_prompt_kernel(bias_ref, q_ref, k_ref, v_ref, km_ref, vm_ref, gate_ref, o_ref):
    p = pl.program_id(1)
    qi = pl.program_id(2)
    q = q_ref[0]
    lane = lax.broadcasted_iota(jnp.int32, (1, 2 * SB_DH), 1)
    r = lax.broadcasted_iota(jnp.int32, (SB_BLK, SB_BLK), 0)
    s = lax.broadcasted_iota(jnp.int32, (SB_BLK, SB_BLK), 1)
    ustrict = (r > s).astype(BF16)
    diag_mask = s < r
    rm = lax.broadcasted_iota(jnp.int32, (PAGE, PAGE), 0)
    sm = lax.broadcasted_iota(jnp.int32, (PAGE, PAGE), 1)
    ustrict_m = (rm > sm).astype(BF16)
    meta_mask = lax.broadcasted_iota(jnp.int32, (SB_BLK, PAGE), 1) < N_META

    out = jnp.zeros((SB_BLK, 2 * SB_DH), F32)
    for hh in range(2):
        hmask = (lane // SB_DH) == hh
        bias = bias_ref[2 * p + hh]
        qh = jnp.where(hmask, q, 0.0).astype(BF16)

        def vmasked(vb):
            return jnp.where(hmask, vb, 0.0).astype(BF16)

        row0 = pl.multiple_of(qi * SB_BLK, SB_BLK)
        carry = jnp.zeros((SB_BLK, 1), F32)
        acc = jnp.zeros((SB_BLK, 2 * SB_DH), F32)
        carry, acc = _sb_block(qh, k_ref[0, pl.ds(row0, SB_BLK), :],
                               vmasked(v_ref[0, pl.ds(row0, SB_BLK), :]),
                               bias, carry, acc, diag_mask, ustrict)

        def body(n, ca):
            start = pl.multiple_of((qi - 1 - n) * SB_BLK, SB_BLK)
            return _sb_block(qh, k_ref[0, pl.ds(start, SB_BLK), :],
                             vmasked(v_ref[0, pl.ds(start, SB_BLK), :]),
                             bias, ca[0], ca[1], None, ustrict)

        carry, acc = lax.fori_loop(0, qi, body, (carry, acc))
        carry, acc = _sb_block(qh, km_ref[...], vmasked(vm_ref[...]),
                               bias, carry, acc, meta_mask, ustrict_m)
        out = out + acc
    o_ref[0] = out * gate_ref[0]


def _sb_decode_kernel(pt_ref, q_ref, gate_ref, bias_ref, *refs):
    del pt_ref
    k_refs = refs[:DEC_PAGES]
    v_refs = refs[DEC_PAGES:2 * DEC_PAGES]
    o_ref, acc_ref, carry_ref = refs[2 * DEC_PAGES:]
    ci = pl.program_id(1)

    @pl.when(ci == 0)
    def _():
        acc_ref[...] = jnp.zeros_like(acc_ref)
        carry_ref[...] = jnp.zeros_like(carry_ref)

    width = SB_HEADS * SB_DH
    hrow = lax.broadcasted_iota(jnp.int32, (SB_HEADS, width), 0)
    hlane = lax.broadcasted_iota(jnp.int32, (SB_HEADS, width), 1) // SB_DH
    own = hrow == hlane
    qbd = jnp.where(own, jnp.broadcast_to(q_ref[0], (SB_HEADS, width)), 0.0).astype(BF16)
    r = lax.broadcasted_iota(jnp.int32, (PAGE, PAGE), 0)
    s = lax.broadcasted_iota(jnp.int32, (PAGE, PAGE), 1)
    ustrict = (r > s).astype(BF16)
    bias = bias_ref[...]

    carry = carry_ref[...]
    acc = acc_ref[...]
    for i in range(DEC_PAGES):
        z = _dot(qbd, k_refs[i][0].astype(BF16)) * (SB_DH ** -0.5) + bias
        sp = _softplus(z)
        lk = -sp
        hi, lo = _split2(lk)
        cum = _dot(hi, ustrict) + _dot(lo, ustrict)
        a = jnp.exp((z - sp) + cum + carry)
        acc = acc + _dot_nt(a.astype(BF16), v_refs[i][0].astype(BF16))
        carry = carry + jnp.sum(lk, axis=-1, keepdims=True)
    acc_ref[...] = acc
    carry_ref[...] = carry

    @pl.when(ci == pl.num_programs(1) - 1)
    def _():
        o = jnp.sum(jnp.where(own, acc, 0.0), axis=0, keepdims=True)
        o_ref[0] = o * gate_ref[0]


def _merge_kernel(x_ref, a_ref, b_ref, w_ref, y_ref):
    half = a_ref.shape[-1]
    y = _dot(a_ref[...].astype(BF16), w_ref[0:half, :])
    y = y + _dot(b_ref[...].astype(BF16), w_ref[half:2 * half, :])
    y_ref[...] = x_ref[...] + y


def _merge(x2d, a, b, w_bf16, tm):
    rows, d = x2d.shape
    half = a.shape[-1]
    return pl.pallas_call(
        _merge_kernel,
        grid=(rows // tm,),
        in_specs=[
            pl.BlockSpec((tm, d), lambda i: (i, 0)),
            pl.BlockSpec((tm, half), lambda i: (i, 0)),
            pl.BlockSpec((tm, half), lambda i: (i, 0)),
            pl.BlockSpec(w_bf16.shape, lambda i: (0, 0)),
        ],
        out_specs=pl.BlockSpec((tm, d), lambda i: (i, 0)),
        out_shape=jax.ShapeDtypeStruct((rows, d), F32),
        compiler_params=pltpu.CompilerParams(
            dimension_semantics=("parallel",), vmem_limit_bytes=VMEM_LIMIT),
        name="merge",
    )(x2d, a, b, w_bf16)


def kernel(x_prompt, x_sample, cache_k, cache_v, state_hgrn, page_table, meta_tokens, norm_g,
           w_in, lb_logits, hg_norm_g, q_norm_g, k_norm_g, sb_bias, w_out):
    nb, seq, d = x_prompt.shape
    db = x_sample.shape[0]
    n_pages = page_table.shape[1]
    n_phys = cache_k.shape[1]
    assert w_in.shape[0] == 1 and w_in.shape[2] == N_SEG * SEG
    assert seq % SB_BLK == 0 and seq % HG_CHUNK == 0 and n_pages % DEC_PAGES == 0
    assert db % STEP_SAMPLES == 0 and (nb * seq) % PROJ_ROWS == 0

    w_in_b = w_in[0].astype(BF16)
    w_out_b = w_out[0].astype(BF16)
    g_row = norm_g[0][None, :]
    qg = jnp.tile(q_norm_g[0], SB_HEADS)[None, :]
    kg = jnp.tile(k_norm_g[0], SB_HEADS)[None, :]
    ng = hg_norm_g[0].reshape(1, HG_HEADS * HG_DIM)
    lane_head = jnp.arange(SEG) // SB_DH
    grp = (lane_head[:, None] == lane_head[None, :]).astype(BF16)
    bias = sb_bias[0].astype(F32)

    x2d = x_prompt.reshape(nb * seq, d)
    hq, lf, hv, hgate, sq, sk, sv, sgate = _project(x2d, PROJ_ROWS, g_row, w_in_b, lb_logits, qg, kg, grp)
    xs2d = x_sample.reshape(db, d)
    small = jnp.concatenate([xs2d, meta_tokens.astype(F32)], axis=0)
    n_small = small.shape[0]
    s_out = _project(small, n_small, g_row, w_in_b, lb_logits, qg, kg, grp)
    hq_s, lf_s, hv_s, hgate_s, sq_s, sk_s, sv_s, sgate_s = [a[:db] for a in s_out]
    _, lf_m, hv_m, _, _, sk_m, sv_m, _ = [a[db:] for a in s_out]

    pad_m = ((0, HG_CHUNK - N_META), (0, 0))
    st_meta = pl.pallas_call(
        _hgrn_meta_kernel,
        out_shape=jax.ShapeDtypeStruct((HG_HEADS, HG_DIM, HG_DIM), F32),
        name="hgrn_meta",
    )(jnp.pad(lf_m, pad_m), jnp.pad(hv_m, pad_m))

    def r3(a):
        return a.reshape(nb, seq, SEG)

    n_chunks = seq // HG_CHUNK
    cspec = pl.BlockSpec((1, HG_CHUNK, SEG), lambda b, c: (b, c, 0))
    a_hg, s_prompt = pl.pallas_call(
        _hgrn_prompt_kernel,
        grid=(nb, n_chunks),
        in_specs=[cspec, cspec, cspec, cspec,
                  pl.BlockSpec((1, SEG), lambda b, c: (0, 0)),
                  pl.BlockSpec((HG_HEADS, HG_DIM, HG_DIM), lambda b, c: (0, 0, 0))],
        out_specs=[cspec,
                   pl.BlockSpec((1, HG_HEADS, HG_DIM, HG_DIM), lambda b, c: (b, 0, 0, 0))],
        out_shape=[jax.ShapeDtypeStruct((nb, seq, SEG), F32),
                   jax.ShapeDtypeStruct((nb, HG_HEADS, HG_DIM, HG_DIM), F32)],
        scratch_shapes=[pltpu.VMEM((HG_HEADS, HG_DIM, HG_DIM), F32)],
        compiler_params=pltpu.CompilerParams(
            dimension_semantics=("parallel", "arbitrary"), vmem_limit_bytes=VMEM_LIMIT),
        name="hgrn_prompt",
    )(r3(hq), r3(lf), r3(hv), r3(hgate), ng, st_meta)

    rspec = pl.BlockSpec((STEP_SAMPLES, SEG), lambda i: (i, 0))
    sspec = pl.BlockSpec((STEP_SAMPLES, HG_HEADS, HG_DIM, HG_DIM), lambda i: (i, 0, 0, 0))
    s_sample, a_hg_s = pl.pallas_call(
        _hgrn_step_kernel,
        grid=(db // STEP_SAMPLES,),
        in_specs=[rspec, rspec, rspec, rspec, pl.BlockSpec((1, SEG), lambda i: (0, 0)), sspec],
        out_specs=[sspec, rspec],
        out_shape=[jax.ShapeDtypeStruct((db, HG_HEADS, HG_DIM, HG_DIM), F32),
                   jax.ShapeDtypeStruct((db, SEG), F32)],
        compiler_params=pltpu.CompilerParams(
            dimension_semantics=("parallel",), vmem_limit_bytes=VMEM_LIMIT),
        name="hgrn_step",
    )(hq_s, lf_s, hv_s, hgate_s, ng, state_hgrn[0])

    pad_k = ((0, PAGE - N_META), (0, 0))
    n_qb = seq // SB_BLK
    n_pairs = SB_HEADS // 2
    qspec = pl.BlockSpec((1, SB_BLK, 2 * SB_DH), lambda b, p, i, *_: (b, i, p))
    kvspec = pl.BlockSpec((1, seq, 2 * SB_DH), lambda b, p, i, *_: (b, 0, p))
    mspec = pl.BlockSpec((PAGE, 2 * SB_DH), lambda b, p, i, *_: (0, p))
    b_sb = pl.pallas_call(
        _sb_prompt_kernel,
        grid_spec=pltpu.PrefetchScalarGridSpec(
            num_scalar_prefetch=1,
            grid=(nb, n_pairs, n_qb),
            in_specs=[qspec, kvspec, kvspec, mspec, mspec, qspec],
            out_specs=qspec,
        ),
        out_shape=jax.ShapeDtypeStruct((nb, seq, SEG), F32),
        compiler_params=pltpu.CompilerParams(
            dimension_semantics=("parallel", "parallel", "arbitrary"), vmem_limit_bytes=VMEM_LIMIT),
        name="sb_prompt",
    )(bias, r3(sq), r3(sk), r3(sv), jnp.pad(sk_m, pad_k), jnp.pad(sv_m, pad_k), r3(sgate))

    ck = jnp.transpose(cache_k[0], (0, 2, 3, 1)).reshape(n_phys, SEG, PAGE)
    cv = jnp.transpose(cache_v[0], (0, 2, 3, 1)).reshape(n_phys, SEG, PAGE)

    def page_spec(slot):
        return pl.BlockSpec(
            (1, SEG, PAGE),
            lambda b, c, pt, slot=slot: (pt[b, n_pages - 1 - (c * DEC_PAGES + slot)], 0, 0))

    vec_spec = pl.BlockSpec((1, 1, SEG), lambda b, c, pt: (b, 0, 0))
    bias_bc = jnp.broadcast_to(bias[:, None], (SB_HEADS, PAGE))
    b_sb_s = pl.pallas_call(
        _sb_decode_kernel,
        grid_spec=pltpu.PrefetchScalarGridSpec(
            num_scalar_prefetch=1,
            grid=(db, n_pages // DEC_PAGES),
            in_specs=[vec_spec, vec_spec, pl.BlockSpec((SB_HEADS, PAGE), lambda b, c, pt: (0, 0))]
                     + [page_spec(i) for i in range(DEC_PAGES)]
                     + [page_spec(i) for i in range(DEC_PAGES)],
            out_specs=vec_spec,
            scratch_shapes=[pltpu.VMEM((SB_HEADS, SEG), F32), pltpu.VMEM((SB_HEADS, PAGE), F32)],
        ),
        out_shape=jax.ShapeDtypeStruct((db, 1, SEG), F32),
        compiler_params=pltpu.CompilerParams(
            dimension_semantics=("parallel", "arbitrary"), vmem_limit_bytes=VMEM_LIMIT),
        name="sb_decode",
    )(page_table, sq_s.reshape(db, 1, SEG), sgate_s.reshape(db, 1, SEG), bias_bc,
      *([ck] * DEC_PAGES), *([cv] * DEC_PAGES))

    y_prompt = _merge(x2d, a_hg.reshape(nb * seq, SEG), b_sb.reshape(nb * seq, SEG), w_out_b, PROJ_ROWS)
    y_sample = _merge(xs2d, a_hg_s, b_sb_s.reshape(db, SEG), w_out_b, db)

    def with_meta(meta_rows, real):
        m = jnp.broadcast_to(meta_rows[None], (nb, N_META, SEG))
        return jnp.concatenate([m, r3(real)], axis=1).reshape(1, nb, N_META + seq, SB_HEADS, SB_DH)

    return (y_prompt.reshape(nb, seq, d),
            y_sample.reshape(db, 1, d),
            with_meta(sk_m, sk),
            with_meta(sv_m, sv),
            s_prompt[None],
            sk_s.reshape(1, db, 1, SB_HEADS, SB_DH),
            sv_s.reshape(1, db, 1, SB_HEADS, SB_DH),
            s_sample[None])
```

```python
import functools

import jax
import jax.numpy as jnp
from jax import lax
from jax.experimental import pallas as pl
from jax.experimental.pallas import tpu as pltpu

F32 = jnp.float32
BF16 = jnp.bfloat16
EPS = 1e-6

SEG = 512
N_SEG = 8
HG_HEADS = 4
HG_DIM = 128
SB_HEADS = 8
SB_DH = 64
PAGE = 128
N_META = 16

HG_CHUNK = 128
SB_BLK = 256
PROJ_ROWS = 256
DEC_PAGES = 16
STEP_SAMPLES = 8

VMEM_LIMIT = 48 * 1024 * 1024

NT_DIMS = (((1,), (1,)), ((), ()))


def _dot(a, b):
    return jnp.dot(a, b, preferred_element_type=F32)


def _dot_nt(a, b):
    return lax.dot_general(a, b, NT_DIMS, preferred_element_type=F32)


def _split2(x):
    hi = x.astype(BF16)
    lo = (x - hi.astype(F32)).astype(BF16)
    return hi, lo


def _split3(x):
    hi = x.astype(BF16)
    r = x - hi.astype(F32)
    mid = r.astype(BF16)
    lo = (r - mid.astype(F32)).astype(BF16)
    return hi, mid, lo


def _split2_trunc(x):
    bits = lax.bitcast_convert_type(x, jnp.uint32) & jnp.uint32(0xFFFF0000)
    hi = lax.bitcast_convert_type(bits, F32)
    return hi.astype(BF16), (x - hi).astype(BF16)


def _softplus(z):
    neg_abs = lax.bitcast_convert_type(
        lax.bitcast_convert_type(z, jnp.uint32) | jnp.uint32(0x80000000), F32)
    return jnp.maximum(z, 0.0) + jnp.log(1.0 + jnp.exp(neg_abs))


def _silu(z):
    return z * (1.0 / (1.0 + jnp.exp(-z)))


def _proj_kernel(x_ref, g_ref, w_ref, lbl_ref, qg_ref, kg_ref, grp_ref,
                 hq_ref, lf_ref, hv_ref, hgate_ref, sq_ref, sk_ref, sv_ref, sgate_ref):
    x = x_ref[...]
    ms = jnp.mean(x * x, axis=-1, keepdims=True)
    h = (x * lax.rsqrt(ms + EPS) * g_ref[...]).astype(BF16)

    def seg(j):
        return _dot(h, w_ref[:, j * SEG:(j + 1) * SEG])

    def group_rms(u, gain):
        hi, lo = _split2(u * u)
        ss = _dot(hi, grp_ref[...]) + _dot(lo, grp_ref[...])
        return u * lax.rsqrt(ss * (1.0 / SB_DH) + EPS) * gain

    hq_ref[...] = _silu(seg(0))

    l = lbl_ref[...]
    e = jnp.exp(l - jnp.max(l, axis=0, keepdims=True))
    lb = e[0:1, :] / jnp.sum(e, axis=0, keepdims=True)
    a = jnp.log(lb)
    z = seg(1)
    bb = jnp.log1p(-lb) + (jnp.minimum(z, 0.0) - jnp.log1p(jnp.exp(-jnp.abs(z))))
    lf_ref[...] = jnp.maximum(a, bb) + jnp.log1p(jnp.exp(-jnp.abs(a - bb)))

    hv_ref[...] = seg(2)
    hgate_ref[...] = _silu(seg(3))
    sq_ref[...] = group_rms(seg(4), qg_ref[...])
    sk_ref[...] = group_rms(seg(5), kg_ref[...])
    sv_ref[...] = seg(6)
    sgate_ref[...] = _silu(seg(7))


def _project(x2d, tm, norm_g, w_bf16, lb_logits, qg, kg, grp):
    rows, d = x2d.shape
    const = lambda i: (0, 0)
    row_spec = pl.BlockSpec((tm, SEG), lambda i: (i, 0))
    return pl.pallas_call(
        _proj_kernel,
        grid=(rows // tm,),
        in_specs=[
            pl.BlockSpec((tm, d), lambda i: (i, 0)),
            pl.BlockSpec((1, d), const),
            pl.BlockSpec(w_bf16.shape, const),
            pl.BlockSpec(lb_logits.shape, const),
            pl.BlockSpec((1, SEG), const),
            pl.BlockSpec((1, SEG), const),
            pl.BlockSpec((SEG, SEG), const),
        ],
        out_specs=[row_spec] * N_SEG,
        out_shape=[jax.ShapeDtypeStruct((rows, SEG), F32)] * N_SEG,
        compiler_params=pltpu.CompilerParams(
            dimension_semantics=("parallel",), vmem_limit_bytes=VMEM_LIMIT),
        name="proj",
    )(x2d, norm_g, w_bf16, lb_logits, qg, kg, grp)


def _hgrn_chunk(q, lf, v, st, want_out):
    c = q.shape[0]
    width = q.shape[1]
    r = lax.broadcasted_iota(jnp.int32, (c, c), 0)
    s = lax.broadcasted_iota(jnp.int32, (c, c), 1)
    tri = (s <= r).astype(BF16)
    b = sum(_dot(tri, p) for p in _split3(lf))
    kk = 1.0 - jnp.exp(lf)
    b_last = b[c - 1:c, :]
    kdec = (kk * jnp.exp(jnp.minimum(b_last - b, 0.0))).astype(BF16)
    sdec = jnp.exp(b_last)

    new_st = []
    for h in range(HG_HEADS):
        sl = slice(h * HG_DIM, (h + 1) * HG_DIM)
        vt = v[:, sl].T.astype(BF16)
        new_st.append(st[h] * sdec[:, sl] + _dot(vt, kdec[:, sl]))
    if not want_out:
        return new_st, None

    levels = [(0, q, kk)]
    m = c // 2
    while m >= 8:
        nb = c // m
        starts = [b[i * m:i * m + 1, :] for i in range(nb)]
        hold = jnp.concatenate([jnp.broadcast_to(starts[i], (m, width)) for i in range(nb)], axis=0)
        nxt = jnp.concatenate(
            [jnp.broadcast_to(starts[min(i + 1, nb - 1)], (m, width)) for i in range(nb)], axis=0)
        levels.append((m, q * jnp.exp(jnp.minimum(b - hold, 0.0)),
                       kk * jnp.exp(jnp.minimum(nxt - b, 0.0))))
        m //= 2
    ri = lax.broadcasted_iota(jnp.int32, (c, width), 0)
    r4 = ri & 3
    r2 = ri & 1
    dn1 = pltpu.roll(lf, 1, 0)
    dn2 = pltpu.roll(lf, 2, 0)
    up1 = pltpu.roll(lf, c - 1, 0)
    up2 = pltpu.roll(lf, c - 2, 0)
    up3 = pltpu.roll(lf, c - 3, 0)
    up4 = pltpu.roll(lf, c - 4, 0)
    zero = jnp.zeros_like(lf)
    eq4 = jnp.where(r4 > 0, lf, zero) + jnp.where(r4 > 1, dn1, zero) + jnp.where(r4 > 2, dn2, zero)
    ek4 = up1 + jnp.where(r4 <= 2, up2, zero) + jnp.where(r4 <= 1, up3, zero) + jnp.where(r4 == 0, up4, zero)
    eq2 = jnp.where(r2 > 0, lf, zero)
    ek2 = up1 + jnp.where(r2 == 0, up2, zero)
    levels.append((4, q * jnp.exp(eq4), kk * jnp.exp(ek4)))
    levels.append((2, q * jnp.exp(eq2), kk * jnp.exp(ek2)))
    levels.append((1, q, kk * jnp.exp(up1)))

    x = r ^ s
    lower = s < r
    masks = [r == s if m == 0 else lower & (x >= m) & (x < 2 * m) for m, _, _ in levels]
    qdec = (q * jnp.exp(b)).astype(BF16)
    levels = [(qm.astype(BF16), km.astype(BF16)) for _, qm, km in levels]

    outs = []
    for h in range(HG_HEADS):
        sl = slice(h * HG_DIM, (h + 1) * HG_DIM)
        a = jnp.zeros((c, c), F32)
        for mask, (qm, km) in zip(masks, levels):
            a = jnp.where(mask, _dot_nt(qm[:, sl], km[:, sl]), a)
        o = _dot_nt(qdec[:, sl], st[h].astype(BF16)) + _dot(a.astype(BF16), v[:, sl].astype(BF16))
        outs.append(o)
    return new_st, outs


def _hgrn_meta_kernel(lf_ref, v_ref, st_ref):
    st0 = [jnp.zeros((HG_DIM, HG_DIM), F32)] * HG_HEADS
    lf = lf_ref[...]
    new_st, _ = _hgrn_chunk(lf, lf, v_ref[...], st0, False)
    for h in range(HG_HEADS):
        st_ref[h] = new_st[h]


def _hgrn_prompt_kernel(q_ref, lf_ref, v_ref, gate_ref, ng_ref, st0_ref, o_ref, s_ref, st_ref):
    ci = pl.program_id(1)

    @pl.when(ci == 0)
    def _():
        st_ref[...] = st0_ref[...]

    st = [st_ref[h] for h in range(HG_HEADS)]
    new_st, outs = _hgrn_chunk(q_ref[0], lf_ref[0], v_ref[0], st, True)
    for h in range(HG_HEADS):
        sl = slice(h * HG_DIM, (h + 1) * HG_DIM)
        st_ref[h] = new_st[h]
        o = outs[h]
        ms = jnp.mean(o * o, axis=-1, keepdims=True)
        o_ref[0, :, sl] = o * lax.rsqrt(ms + EPS) * ng_ref[:, sl] * gate_ref[0, :, sl]

    @pl.when(ci == pl.num_programs(1) - 1)
    def _():
        for h in range(HG_HEADS):
            s_ref[0, h] = new_st[h].T


def _hgrn_step_kernel(q_ref, lf_ref, v_ref, gate_ref, ng_ref, s0_ref, s_ref, a_ref):
    for i in range(STEP_SAMPLES):
        q = q_ref[i:i + 1, :]
        f = jnp.exp(lf_ref[i:i + 1, :])
        v = v_ref[i:i + 1, :]
        qk = q * (1.0 - f)
        for h in range(HG_HEADS):
            sl = slice(h * HG_DIM, (h + 1) * HG_DIM)
            fc = jnp.broadcast_to(f[:, sl], (HG_DIM, HG_DIM)).T
            sn = fc * s0_ref[i, h] + (1.0 - fc) * v[:, sl]
            s_ref[i, h] = sn
            qf = jnp.broadcast_to(q[:, sl] * f[:, sl], (8, HG_DIM)).astype(BF16)
            o = _dot(qf, s0_ref[i, h].astype(BF16))[0:1, :]
            o = o + jnp.sum(qk[:, sl], axis=-1, keepdims=True) * v[:, sl]
            ms = jnp.mean(o * o, axis=-1, keepdims=True)
            a_ref[i:i + 1, sl] = o * lax.rsqrt(ms + EPS) * ng_ref[:, sl] * gate_ref[i:i + 1, sl]


def _sb_block(q2, kb, vbs, biases, carries, acc, mask, neg_from):
    tq = q2.shape[0] // 2
    z2 = _dot_nt(q2, kb)
    new_carries = []
    for h in range(2):
        z = z2[h * tq:(h + 1) * tq] + biases[h]
        sp = _softplus(z)
        if mask is not None:
            sp = jnp.where(mask, sp, 0.0)
        hi, lo = _split2_trunc(sp)
        tail = _dot(hi, neg_from) + _dot(lo, neg_from)
        a = jnp.exp(z + tail + carries[h])
        if mask is not None:
            a = jnp.where(mask, a, 0.0)
        acc = acc + _dot(a.astype(BF16), vbs[h])
        new_carries.append(carries[h] - jnp.sum(sp, axis=-1, keepdims=True))
    return tuple(new_carries), acc


def _sb_prompt_kernel(bias_ref, q_ref, k_ref, v_ref, km_ref, vm_ref, gate_ref, o_ref,
                      kb_ref, v0_ref, v1_ref, kmb_ref, vm0_ref, vm1_ref):
    p = pl.program_id(1)
    qi = pl.program_id(2)
    lane = lax.broadcasted_iota(jnp.int32, (1, 2 * SB_DH), 1)
    head0 = lane < SB_DH

    @pl.when(qi == 0)
    def _():
        kb_ref[...] = k_ref[0].astype(BF16)
        v = v_ref[0]
        v0_ref[...] = jnp.where(head0, v, 0.0).astype(BF16)
        v1_ref[...] = jnp.where(head0, 0.0, v).astype(BF16)
        kmb_ref[...] = km_ref[...].astype(BF16)
        vm = vm_ref[...]
        vm0_ref[...] = jnp.where(head0, vm, 0.0).astype(BF16)
        vm1_ref[...] = jnp.where(head0, 0.0, vm).astype(BF16)

    q = q_ref[0] * (SB_DH ** -0.5)
    q2 = jnp.concatenate([jnp.where(head0, q, 0.0), jnp.where(head0, 0.0, q)], axis=0).astype(BF16)
    biases = (bias_ref[2 * p], bias_ref[2 * p + 1])
    r = lax.broadcasted_iota(jnp.int32, (SB_BLK, SB_BLK), 0)
    s = lax.broadcasted_iota(jnp.int32, (SB_BLK, SB_BLK), 1)
    neg_from = jnp.where(r >= s, -1.0, 0.0).astype(BF16)
    diag_mask = s < r
    rm = lax.broadcasted_iota(jnp.int32, (PAGE, PAGE), 0)
    sm = lax.broadcasted_iota(jnp.int32, (PAGE, PAGE), 1)
    neg_from_m = jnp.where(rm >= sm, -1.0, 0.0).astype(BF16)
    meta_mask = lax.broadcasted_iota(jnp.int32, (SB_BLK, PAGE), 1) < N_META

    def block_at(start, carries, acc, mask):
        rows = pl.ds(start, SB_BLK)
        return _sb_block(q2, kb_ref[rows, :], (v0_ref[rows, :], v1_ref[rows, :]),
                         biases, carries, acc, mask, neg_from)

    zero_c = jnp.zeros((SB_BLK, 1), F32)
    carries, acc = block_at(pl.multiple_of(qi * SB_BLK, SB_BLK), (zero_c, zero_c),
                            jnp.zeros((SB_BLK, 2 * SB_DH), F32), diag_mask)

    def body(n, state):
        return block_at(pl.multiple_of((qi - 1 - n) * SB_BLK, SB_BLK), state[0], state[1], None)

    carries, acc = lax.fori_loop(0, qi, body, (carries, acc))
    carries, acc = _sb_block(q2, kmb_ref[...], (vm0_ref[...], vm1_ref[...]),
                             biases, carries, acc, meta_mask, neg_from_m)
    o_ref[0] = acc * gate_ref[0]


def _sb_decode_kernel(pt_ref, q_ref, gate_ref, bias_ref, *refs):
    del pt_ref
    k_refs = refs[:DEC_PAGES]
    v_refs = refs[DEC_PAGES:2 * DEC_PAGES]
    o_ref, acc_ref, carry_ref = refs[2 * DEC_PAGES:]
    ci = pl.program_id(1)

    @pl.when(ci == 0)
    def _():
        acc_ref[...] = jnp.zeros_like(acc_ref)
        carry_ref[...] = jnp.zeros_like(carry_ref)

    width = SB_HEADS * SB_DH
    hrow = lax.broadcasted_iota(jnp.int32, (SB_HEADS, width), 0)
    hlane = lax.broadcasted_iota(jnp.int32, (SB_HEADS, width), 1) // SB_DH
    own = hrow == hlane
    q_scaled = jnp.broadcast_to(q_ref[0] * (SB_DH ** -0.5), (SB_HEADS, width))
    qbd = jnp.where(own, q_scaled, 0.0).astype(BF16)
    r = lax.broadcasted_iota(jnp.int32, (PAGE, PAGE), 0)
    s = lax.broadcasted_iota(jnp.int32, (PAGE, PAGE), 1)
    neg_from = jnp.where(r >= s, -1.0, 0.0).astype(BF16)

    def lanes(x, i):
        return x[:, i * PAGE:(i + 1) * PAGE]

    k_all = jnp.concatenate([k_refs[i][0].astype(BF16) for i in range(DEC_PAGES)], axis=1)
    z = _dot(qbd, k_all) + bias_ref[...]
    sp = _softplus(z)
    sp_rows = jnp.concatenate([lanes(sp, i) for i in range(DEC_PAGES)], axis=0)
    hi, lo = _split2_trunc(sp_rows)
    tail = _dot(hi, neg_from) + _dot(lo, neg_from)
    tot = jnp.sum(sp_rows, axis=-1, keepdims=True)
    carry = carry_ref[...]
    parts = []
    for i in range(DEC_PAGES):
        rows = slice(i * SB_HEADS, (i + 1) * SB_HEADS)
        parts.append(jnp.exp(lanes(z, i) + tail[rows] + carry))
        carry = carry - tot[rows]
    a = jnp.concatenate(parts, axis=1).astype(BF16)
    v_all = jnp.concatenate([v_refs[i][0].astype(BF16) for i in range(DEC_PAGES)], axis=1)
    acc = acc_ref[...] + _dot_nt(a, v_all)
    acc_ref[...] = acc
    carry_ref[...] = carry

    @pl.when(ci == pl.num_programs(1) - 1)
    def _():
        o = jnp.sum(jnp.where(own, acc, 0.0), axis=0, keepdims=True)
        o_ref[0] = o * gate_ref[0]


def _merge_kernel(x_ref, a_ref, b_ref, w_ref, y_ref):
    half = a_ref.shape[-1]
    y = _dot(a_ref[...].astype(BF16), w_ref[0:half, :])
    y = y + _dot(b_ref[...].astype(BF16), w_ref[half:2 * half, :])
    y_ref[...] = x_ref[...] + y


def _merge(x2d, a, b, w_bf16, tm):
    rows, d = x2d.shape
    half = a.shape[-1]
    return pl.pallas_call(
        _merge_kernel,
        grid=(rows // tm,),
        in_specs=[
            pl.BlockSpec((tm, d), lambda i: (i, 0)),
            pl.BlockSpec((tm, half), lambda i: (i, 0)),
            pl.BlockSpec((tm, half), lambda i: (i, 0)),
            pl.BlockSpec(w_bf16.shape, lambda i: (0, 0)),
        ],
        out_specs=pl.BlockSpec((tm, d), lambda i: (i, 0)),
        out_shape=jax.ShapeDtypeStruct((rows, d), F32),
        compiler_params=pltpu.CompilerParams(
            dimension_semantics=("parallel",), vmem_limit_bytes=VMEM_LIMIT),
        name="merge",
    )(x2d, a, b, w_bf16)


def kernel(x_prompt, x_sample, cache_k, cache_v, state_hgrn, page_table, meta_tokens, norm_g,
           w_in, lb_logits, hg_norm_g, q_norm_g, k_norm_g, sb_bias, w_out):
    nb, seq, d = x_prompt.shape
    db = x_sample.shape[0]
    n_pages = page_table.shape[1]
    n_phys = cache_k.shape[1]
    assert w_in.shape[0] == 1 and w_in.shape[2] == N_SEG * SEG
    assert seq % SB_BLK == 0 and seq % HG_CHUNK == 0 and n_pages % DEC_PAGES == 0
    assert db % STEP_SAMPLES == 0 and (nb * seq) % PROJ_ROWS == 0

    w_in_b = w_in[0].astype(BF16)
    w_out_b = w_out[0].astype(BF16)
    g_row = norm_g[0][None, :]
    qg = jnp.tile(q_norm_g[0], SB_HEADS)[None, :]
    kg = jnp.tile(k_norm_g[0], SB_HEADS)[None, :]
    ng = hg_norm_g[0].reshape(1, HG_HEADS * HG_DIM)
    lane_head = jnp.arange(SEG) // SB_DH
    grp = (lane_head[:, None] == lane_head[None, :]).astype(BF16)
    bias = sb_bias[0].astype(F32)

    x2d = x_prompt.reshape(nb * seq, d)
    hq, lf, hv, hgate, sq, sk, sv, sgate = _project(x2d, PROJ_ROWS, g_row, w_in_b, lb_logits, qg, kg, grp)
    xs2d = x_sample.reshape(db, d)
    small = jnp.concatenate([xs2d, meta_tokens.astype(F32)], axis=0)
    n_small = small.shape[0]
    s_out = _project(small, n_small, g_row, w_in_b, lb_logits, qg, kg, grp)
    hq_s, lf_s, hv_s, hgate_s, sq_s, sk_s, sv_s, sgate_s = [a[:db] for a in s_out]
    _, lf_m, hv_m, _, _, sk_m, sv_m, _ = [a[db:] for a in s_out]

    pad_m = ((0, HG_CHUNK - N_META), (0, 0))
    st_meta = pl.pallas_call(
        _hgrn_meta_kernel,
        out_shape=jax.ShapeDtypeStruct((HG_HEADS, HG_DIM, HG_DIM), F32),
        name="hgrn_meta",
    )(jnp.pad(lf_m, pad_m), jnp.pad(hv_m, pad_m))

    def r3(a):
        return a.reshape(nb, seq, SEG)

    n_chunks = seq // HG_CHUNK
    cspec = pl.BlockSpec((1, HG_CHUNK, SEG), lambda b, c: (b, c, 0))
    a_hg, s_prompt = pl.pallas_call(
        _hgrn_prompt_kernel,
        grid=(nb, n_chunks),
        in_specs=[cspec, cspec, cspec, cspec,
                  pl.BlockSpec((1, SEG), lambda b, c: (0, 0)),
                  pl.BlockSpec((HG_HEADS, HG_DIM, HG_DIM), lambda b, c: (0, 0, 0))],
        out_specs=[cspec,
                   pl.BlockSpec((1, HG_HEADS, HG_DIM, HG_DIM), lambda b, c: (b, 0, 0, 0))],
        out_shape=[jax.ShapeDtypeStruct((nb, seq, SEG), F32),
                   jax.ShapeDtypeStruct((nb, HG_HEADS, HG_DIM, HG_DIM), F32)],
        scratch_shapes=[pltpu.VMEM((HG_HEADS, HG_DIM, HG_DIM), F32)],
        compiler_params=pltpu.CompilerParams(
            dimension_semantics=("parallel", "arbitrary"), vmem_limit_bytes=VMEM_LIMIT),
        name="hgrn_prompt",
    )(r3(hq), r3(lf), r3(hv), r3(hgate), ng, st_meta)

    rspec = pl.BlockSpec((STEP_SAMPLES, SEG), lambda i: (i, 0))
    sspec = pl.BlockSpec((STEP_SAMPLES, HG_HEADS, HG_DIM, HG_DIM), lambda i: (i, 0, 0, 0))
    s_sample, a_hg_s = pl.pallas_call(
        _hgrn_step_kernel,
        grid=(db // STEP_SAMPLES,),
        in_specs=[rspec, rspec, rspec, rspec, pl.BlockSpec((1, SEG), lambda i: (0, 0)), sspec],
        out_specs=[sspec, rspec],
        out_shape=[jax.ShapeDtypeStruct((db, HG_HEADS, HG_DIM, HG_DIM), F32),
                   jax.ShapeDtypeStruct((db, SEG), F32)],
        compiler_params=pltpu.CompilerParams(
            dimension_semantics=("parallel",), vmem_limit_bytes=VMEM_LIMIT),
        name="hgrn_step",
    )(hq_s, lf_s, hv_s, hgate_s, ng, state_hgrn[0])

    pad_k = ((0, PAGE - N_META), (0, 0))
    n_qb = seq // SB_BLK
    n_pairs = SB_HEADS // 2
    qspec = pl.BlockSpec((1, SB_BLK, 2 * SB_DH), lambda b, p, i, *_: (b, i, p))
    kvspec = pl.BlockSpec((1, seq, 2 * SB_DH), lambda b, p, i, *_: (b, 0, p))
    mspec = pl.BlockSpec((PAGE, 2 * SB_DH), lambda b, p, i, *_: (0, p))
    b_sb = pl.pallas_call(
        _sb_prompt_kernel,
        grid_spec=pltpu.PrefetchScalarGridSpec(
            num_scalar_prefetch=1,
            grid=(nb, n_pairs, n_qb),
            in_specs=[qspec, kvspec, kvspec, mspec, mspec, qspec],
            out_specs=qspec,
            scratch_shapes=[pltpu.VMEM((seq, 2 * SB_DH), BF16)] * 3
                           + [pltpu.VMEM((PAGE, 2 * SB_DH), BF16)] * 3,
        ),
        out_shape=jax.ShapeDtypeStruct((nb, seq, SEG), F32),
        compiler_params=pltpu.CompilerParams(
            dimension_semantics=("parallel", "parallel", "arbitrary"), vmem_limit_bytes=VMEM_LIMIT),
        name="sb_prompt",
    )(bias, r3(sq), r3(sk), r3(sv), jnp.pad(sk_m, pad_k), jnp.pad(sv_m, pad_k), r3(sgate))

    ck = jnp.transpose(cache_k[0], (0, 2, 3, 1)).reshape(n_phys, SEG, PAGE)
    cv = jnp.transpose(cache_v[0], (0, 2, 3, 1)).reshape(n_phys, SEG, PAGE)

    def page_spec(slot):
        return pl.BlockSpec(
            (1, SEG, PAGE),
            lambda b, c, pt, slot=slot: (pt[b, n_pages - 1 - (c * DEC_PAGES + slot)], 0, 0))

    vec_spec = pl.BlockSpec((1, 1, SEG), lambda b, c, pt: (b, 0, 0))
    bias_bc = jnp.broadcast_to(bias[:, None], (SB_HEADS, DEC_PAGES * PAGE))
    b_sb_s = pl.pallas_call(
        _sb_decode_kernel,
        grid_spec=pltpu.PrefetchScalarGridSpec(
            num_scalar_prefetch=1,
            grid=(db, n_pages // DEC_PAGES),
            in_specs=[vec_spec, vec_spec,
                      pl.BlockSpec((SB_HEADS, DEC_PAGES * PAGE), lambda b, c, pt: (0, 0))]
                     + [page_spec(i) for i in range(DEC_PAGES)]
                     + [page_spec(i) for i in range(DEC_PAGES)],
            out_specs=vec_spec,
            scratch_shapes=[pltpu.VMEM((SB_HEADS, SEG), F32), pltpu.VMEM((SB_HEADS, PAGE), F32)],
        ),
        out_shape=jax.ShapeDtypeStruct((db, 1, SEG), F32),
        compiler_params=pltpu.CompilerParams(
            dimension_semantics=("parallel", "arbitrary"), vmem_limit_bytes=VMEM_LIMIT),
        name="sb_decode",
    )(page_table, sq_s.reshape(db, 1, SEG), sgate_s.reshape(db, 1, SEG), bias_bc,
      *([ck] * DEC_PAGES), *([cv] * DEC_PAGES))

    y_prompt = _merge(x2d, a_hg.reshape(nb * seq, SEG), b_sb.reshape(nb * seq, SEG), w_out_b, PROJ_ROWS)
    y_sample = _merge(xs2d, a_hg_s, b_sb_s.reshape(db, SEG), w_out_b, db)

    def with_meta(meta_rows, real):
        m = jnp.broadcast_to(meta_rows[None], (nb, N_META, SEG))
        return jnp.concatenate([m, r3(real)], axis=1).reshape(1, nb, N_META + seq, SB_HEADS, SB_DH)

    return (y_prompt.reshape(nb, seq, d),
            y_sample.reshape(db, 1, d),
            with_meta(sk_m, sk),
            with_meta(sv_m, sv),
            s_prompt[None],
            sk_s.reshape(1, db, 1, SB_HEADS, SB_DH),
            sv_s.reshape(1, db, 1, SB_HEADS, SB_DH),
            s_sample[None])
```

```python
import jax
import jax.numpy as jnp
from jax import lax
from jax.experimental import pallas as pl
from jax.experimental.pallas import tpu as pltpu

F32 = jnp.float32
BF16 = jnp.bfloat16
EPS = 1e-6

SEG = 512
N_SEG = 8
HG_HEADS = 4
HG_DIM = 128
SB_HEADS = 8
SB_DH = 64
PAGE = 128
MXU_TILE = 256
N_META = 16

HG_CHUNK = 128
SB_BLK = 256
PROJ_ROWS = 256
DEC_PAGES = 16
STEP_SAMPLES = 8

VMEM_LIMIT = 48 * 1024 * 1024
SB_VMEM_LIMIT = 56 * 1024 * 1024

NT_DIMS = (((1,), (1,)), ((), ()))


def _dot(a, b):
    return jnp.dot(a, b, preferred_element_type=F32)


def _dot_nt(a, b):
    return lax.dot_general(a, b, NT_DIMS, preferred_element_type=F32)


def _split2(x):
    hi = x.astype(BF16)
    lo = (x - hi.astype(F32)).astype(BF16)
    return hi, lo


def _split3(x):
    hi = x.astype(BF16)
    r = x - hi.astype(F32)
    mid = r.astype(BF16)
    lo = (r - mid.astype(F32)).astype(BF16)
    return hi, mid, lo


def _softplus(z):
    return jnp.maximum(z, 0.0) + jnp.log(1.0 + jnp.exp(-jnp.abs(z)))


def _silu(z):
    return z * (1.0 / (1.0 + jnp.exp(-z)))


def _proj_kernel(x_ref, g_ref, w_ref, lbl_ref, qg_ref, kg_ref, grp_ref,
                 hq_ref, lf_ref, hv_ref, hgate_ref, sq_ref, sk_ref, sv_ref, sgate_ref):
    x = x_ref[...]
    ms = jnp.mean(x * x, axis=-1, keepdims=True)
    h = (x * lax.rsqrt(ms + EPS) * g_ref[...]).astype(BF16)

    def seg(j):
        return _dot(h, w_ref[:, j * SEG:(j + 1) * SEG])

    def group_rms(u, gain):
        hi, lo = _split2(u * u)
        gw = grp_ref.shape[0]
        ss = jnp.concatenate(
            [_dot(hi[:, c:c + gw], grp_ref[...]) + _dot(lo[:, c:c + gw], grp_ref[...])
             for c in range(0, SEG, gw)], axis=1)
        return u * lax.rsqrt(ss * (1.0 / SB_DH) + EPS) * gain

    hq_ref[...] = _silu(seg(0))

    l = lbl_ref[...]
    e = jnp.exp(l - jnp.max(l, axis=0, keepdims=True))
    lb = e[0:1, :] / jnp.sum(e, axis=0, keepdims=True)
    a = jnp.log(lb)
    z = seg(1)
    bb = jnp.log1p(-lb) + (jnp.minimum(z, 0.0) - jnp.log1p(jnp.exp(-jnp.abs(z))))
    lf_ref[...] = jnp.maximum(a, bb) + jnp.log1p(jnp.exp(-jnp.abs(a - bb)))

    hv_ref[...] = seg(2)
    hgate_ref[...] = _silu(seg(3))
    sq_ref[...] = group_rms(seg(4), qg_ref[...])
    sk_ref[...] = group_rms(seg(5), kg_ref[...])
    sv_ref[...] = seg(6)
    sgate_ref[...] = _silu(seg(7))


def _project(x2d, tm, norm_g, w_bf16, lb_logits, qg, kg, grp):
    rows, d = x2d.shape
    const = lambda i: (0, 0)
    row_spec = pl.BlockSpec((tm, SEG), lambda i: (i, 0))
    return pl.pallas_call(
        _proj_kernel,
        grid=(rows // tm,),
        in_specs=[
            pl.BlockSpec((tm, d), lambda i: (i, 0)),
            pl.BlockSpec((1, d), const),
            pl.BlockSpec(w_bf16.shape, const),
            pl.BlockSpec(lb_logits.shape, const),
            pl.BlockSpec((1, SEG), const),
            pl.BlockSpec((1, SEG), const),
            pl.BlockSpec(grp.shape, const),
        ],
        out_specs=[row_spec] * N_SEG,
        out_shape=[jax.ShapeDtypeStruct((rows, SEG), F32)] * N_SEG,
        compiler_params=pltpu.CompilerParams(
            dimension_semantics=("parallel",), vmem_limit_bytes=VMEM_LIMIT),
        name="proj",
    )(x2d, norm_g, w_bf16, lb_logits, qg, kg, grp)


def _hgrn_chunk(q, lf, v, st, want_out):
    c = q.shape[0]
    width = q.shape[1]
    r = lax.broadcasted_iota(jnp.int32, (c, c), 0)
    s = lax.broadcasted_iota(jnp.int32, (c, c), 1)
    tri = (s <= r).astype(BF16)
    b = sum(_dot(tri, p) for p in _split3(lf))
    kk = 1.0 - jnp.exp(lf)
    b_last = b[c - 1:c, :]
    kdec = (kk * jnp.exp(jnp.minimum(b_last - b, 0.0))).astype(BF16)
    sdec = jnp.exp(b_last)

    new_st = []
    for h in range(HG_HEADS):
        sl = slice(h * HG_DIM, (h + 1) * HG_DIM)
        vt = v[:, sl].T.astype(BF16)
        new_st.append(st[h] * sdec[:, sl] + _dot(vt, kdec[:, sl]))
    if not want_out:
        return new_st, None

    levels = [(0, q, kk)]
    m = c // 2
    while m >= 8:
        nb = c // m
        starts = [b[i * m:i * m + 1, :] for i in range(nb)]
        hold = jnp.concatenate([jnp.broadcast_to(starts[i], (m, width)) for i in range(nb)], axis=0)
        nxt = jnp.concatenate(
            [jnp.broadcast_to(starts[min(i + 1, nb - 1)], (m, width)) for i in range(nb)], axis=0)
        levels.append((m, q * jnp.exp(jnp.minimum(b - hold, 0.0)),
                       kk * jnp.exp(jnp.minimum(nxt - b, 0.0))))
        m //= 2
    ri = lax.broadcasted_iota(jnp.int32, (c, width), 0)
    r4 = ri & 3
    r2 = ri & 1
    dn1 = pltpu.roll(lf, 1, 0)
    dn2 = pltpu.roll(lf, 2, 0)
    up1 = pltpu.roll(lf, c - 1, 0)
    up2 = pltpu.roll(lf, c - 2, 0)
    up3 = pltpu.roll(lf, c - 3, 0)
    up4 = pltpu.roll(lf, c - 4, 0)
    zero = jnp.zeros_like(lf)
    eq4 = jnp.where(r4 > 0, lf, zero) + jnp.where(r4 > 1, dn1, zero) + jnp.where(r4 > 2, dn2, zero)
    ek4 = up1 + jnp.where(r4 <= 2, up2, zero) + jnp.where(r4 <= 1, up3, zero) + jnp.where(r4 == 0, up4, zero)
    eq2 = jnp.where(r2 > 0, lf, zero)
    ek2 = up1 + jnp.where(r2 == 0, up2, zero)
    levels.append((4, q * jnp.exp(eq4), kk * jnp.exp(ek4)))
    levels.append((2, q * jnp.exp(eq2), kk * jnp.exp(ek2)))
    levels.append((1, q, kk * jnp.exp(up1)))

    x = r ^ s
    lower = s < r
    masks = [r == s if m == 0 else lower & (x >= m) & (x < 2 * m) for m, _, _ in levels]
    qdec = (q * jnp.exp(b)).astype(BF16)
    levels = [(qm.astype(BF16), km.astype(BF16)) for _, qm, km in levels]

    outs = []
    for h in range(HG_HEADS):
        sl = slice(h * HG_DIM, (h + 1) * HG_DIM)
        a = jnp.zeros((c, c), F32)
        for mask, (qm, km) in zip(masks, levels):
            a = jnp.where(mask, _dot_nt(qm[:, sl], km[:, sl]), a)
        o = _dot_nt(qdec[:, sl], st[h].astype(BF16)) + _dot(a.astype(BF16), v[:, sl].astype(BF16))
        outs.append(o)
    return new_st, outs


def _hgrn_meta_kernel(lf_ref, v_ref, st_ref):
    st0 = [jnp.zeros((HG_DIM, HG_DIM), F32)] * HG_HEADS
    lf = lf_ref[...]
    new_st, _ = _hgrn_chunk(lf, lf, v_ref[...], st0, False)
    for h in range(HG_HEADS):
        st_ref[h] = new_st[h]


def _hgrn_prompt_kernel(q_ref, lf_ref, v_ref, gate_ref, ng_ref, st0_ref, o_ref, s_ref, st_ref):
    ci = pl.program_id(1)

    @pl.when(ci == 0)
    def _():
        st_ref[...] = st0_ref[...]

    st = [st_ref[h] for h in range(HG_HEADS)]
    new_st, outs = _hgrn_chunk(q_ref[0], lf_ref[0], v_ref[0], st, True)
    for h in range(HG_HEADS):
        sl = slice(h * HG_DIM, (h + 1) * HG_DIM)
        st_ref[h] = new_st[h]
        o = outs[h]
        ms = jnp.mean(o * o, axis=-1, keepdims=True)
        o_ref[0, :, sl] = o * lax.rsqrt(ms + EPS) * ng_ref[:, sl] * gate_ref[0, :, sl]

    @pl.when(ci == pl.num_programs(1) - 1)
    def _():
        for h in range(HG_HEADS):
            s_ref[0, h] = new_st[h].T


def _hgrn_step_kernel(q_ref, lf_ref, v_ref, gate_ref, ng_ref, s0_ref, s_ref, a_ref):
    for i in range(STEP_SAMPLES):
        q = q_ref[i:i + 1, :]
        f = jnp.exp(lf_ref[i:i + 1, :])
        v = v_ref[i:i + 1, :]
        qk = q * (1.0 - f)
        for h in range(HG_HEADS):
            sl = slice(h * HG_DIM, (h + 1) * HG_DIM)
            fc = jnp.broadcast_to(f[:, sl], (HG_DIM, HG_DIM)).T
            sn = fc * s0_ref[i, h] + (1.0 - fc) * v[:, sl]
            s_ref[i, h] = sn
            qf = jnp.broadcast_to(q[:, sl] * f[:, sl], (8, HG_DIM)).astype(BF16)
            o = _dot(qf, s0_ref[i, h].astype(BF16))[0:1, :]
            o = o + jnp.sum(qk[:, sl], axis=-1, keepdims=True) * v[:, sl]
            ms = jnp.mean(o * o, axis=-1, keepdims=True)
            a_ref[i:i + 1, sl] = o * lax.rsqrt(ms + EPS) * ng_ref[:, sl] * gate_ref[i:i + 1, sl]


def _sb_block(q2, kb, vbs, biases, carries, acc, mask, neg_from):
    tq = q2.shape[0] // 2
    z2 = _dot_nt(q2, kb)
    new_carries = []
    for h in range(2):
        z = z2[h * tq:(h + 1) * tq] + biases[h]
        sp = _softplus(z)
        if mask is not None:
            sp = jnp.where(mask, sp, 0.0)
        hi, lo = _split2(sp)
        tail = _dot(hi, neg_from) + _dot(lo, neg_from)
        a = jnp.exp(z + tail + carries[h])
        if mask is not None:
            a = jnp.where(mask, a, 0.0)
        acc = acc + _dot(a.astype(BF16), vbs[h])
        new_carries.append(carries[h] - jnp.sum(sp, axis=-1, keepdims=True))
    return tuple(new_carries), acc


def _decode_chunk(qbd, bias, k_pages, v_pages, carry, acc, neg_from):
    n = len(k_pages)

    def lanes(x, i):
        return x[:, i * PAGE:(i + 1) * PAGE]

    k_all = jnp.concatenate([kp.astype(BF16) for kp in k_pages], axis=1)
    z = _dot(qbd, k_all) + bias
    sp = _softplus(z)
    sp_rows = jnp.concatenate([lanes(sp, i) for i in range(n)], axis=0)
    hi, lo = _split2(sp_rows)
    tail = _dot(hi, neg_from) + _dot(lo, neg_from)
    tot = jnp.sum(sp_rows, axis=-1, keepdims=True)
    parts = []
    for i in range(n):
        rows = slice(i * SB_HEADS, (i + 1) * SB_HEADS)
        parts.append(jnp.exp(lanes(z, i) + tail[rows] + carry))
        carry = carry - tot[rows]
    a = jnp.concatenate(parts, axis=1).astype(BF16)
    v_all = jnp.concatenate([vp.astype(BF16) for vp in v_pages], axis=1)
    return carry, acc + _dot_nt(a, v_all)


def _sb_kernel(bias_ref, pt_ref,
               q_ref, k_ref, v_ref, km_ref, vm_ref, gate_ref,
               qd_ref, gated_ref, biasd_ref, ck_ref, cv_ref,
               o_ref, od_ref,
               kb_ref, v0_ref, v1_ref, kmb_ref, vm0_ref, vm1_ref,
               kbuf, vbuf, sem, acc_ref, carry_ref):
    p = pl.program_id(1)
    qi = pl.program_id(2)
    step = (pl.program_id(0) * pl.num_programs(1) + p) * pl.num_programs(2) + qi
    n_steps = pl.num_programs(0) * pl.num_programs(1) * pl.num_programs(2)
    step_pages = kbuf.shape[1]
    n_pages = pt_ref.shape[1]
    steps_per_sample = n_pages // step_pages

    def page_copies(st, slot):
        sample = st // steps_per_sample
        first = (st % steps_per_sample) * step_pages
        copies = []
        for i in range(step_pages):
            page = pt_ref[sample, n_pages - 1 - (first + i)]
            copies.append(pltpu.make_async_copy(ck_ref.at[page], kbuf.at[slot, i], sem.at[0, slot]))
            copies.append(pltpu.make_async_copy(cv_ref.at[page], vbuf.at[slot, i], sem.at[1, slot]))
        return copies

    slot = step % 2

    @pl.when(step == 0)
    def _():
        for c in page_copies(step, slot):
            c.start()

    @pl.when(step + 1 < n_steps)
    def _():
        for c in page_copies(step + 1, 1 - slot):
            c.start()

    lane = lax.broadcasted_iota(jnp.int32, (1, 2 * SB_DH), 1)
    head0 = lane < SB_DH

    @pl.when(qi == 0)
    def _():
        kb_ref[...] = k_ref[0].astype(BF16)
        v = v_ref[0]
        v0_ref[...] = jnp.where(head0, v, 0.0).astype(BF16)
        v1_ref[...] = jnp.where(head0, 0.0, v).astype(BF16)
        kmb_ref[...] = km_ref[...].astype(BF16)
        vm = vm_ref[...]
        vm0_ref[...] = jnp.where(head0, vm, 0.0).astype(BF16)
        vm1_ref[...] = jnp.where(head0, 0.0, vm).astype(BF16)

    q = q_ref[0] * (SB_DH ** -0.5)
    q2 = jnp.concatenate([jnp.where(head0, q, 0.0), jnp.where(head0, 0.0, q)], axis=0).astype(BF16)
    biases = (bias_ref[2 * p], bias_ref[2 * p + 1])
    r = lax.broadcasted_iota(jnp.int32, (SB_BLK, SB_BLK), 0)
    s = lax.broadcasted_iota(jnp.int32, (SB_BLK, SB_BLK), 1)
    neg_from = jnp.where(r >= s, -1.0, 0.0).astype(BF16)
    diag_mask = s < r
    rm = lax.broadcasted_iota(jnp.int32, (PAGE, PAGE), 0)
    sm = lax.broadcasted_iota(jnp.int32, (PAGE, PAGE), 1)
    neg_from_m = jnp.where(rm >= sm, -1.0, 0.0).astype(BF16)
    meta_mask = lax.broadcasted_iota(jnp.int32, (SB_BLK, PAGE), 1) < N_META

    def block_at(start, carries, acc, mask):
        rows = pl.ds(start, SB_BLK)
        return _sb_block(q2, kb_ref[rows, :], (v0_ref[rows, :], v1_ref[rows, :]),
                         biases, carries, acc, mask, neg_from)

    zero_c = jnp.zeros((SB_BLK, 1), F32)
    carries, acc = block_at(pl.multiple_of(qi * SB_BLK, SB_BLK), (zero_c, zero_c),
                            jnp.zeros((SB_BLK, 2 * SB_DH), F32), diag_mask)

    def body(n, state):
        return block_at(pl.multiple_of((qi - 1 - n) * SB_BLK, SB_BLK), state[0], state[1], None)

    carries, acc = lax.fori_loop(0, qi, body, (carries, acc))

    for c in page_copies(step, slot):
        c.wait()
    part = step % steps_per_sample

    @pl.when(part == 0)
    def _():
        acc_ref[...] = jnp.zeros_like(acc_ref)
        carry_ref[...] = jnp.zeros_like(carry_ref)

    width = SB_HEADS * SB_DH
    hrow = lax.broadcasted_iota(jnp.int32, (SB_HEADS, width), 0)
    hlane = lax.broadcasted_iota(jnp.int32, (SB_HEADS, width), 1) // SB_DH
    own = hrow == hlane
    qd = jnp.broadcast_to(qd_ref[0] * (SB_DH ** -0.5), (SB_HEADS, width))
    qbd = jnp.where(own, qd, 0.0).astype(BF16)
    carry_d = carry_ref[...]
    acc_d = acc_ref[...]
    for c0 in range(0, step_pages, DEC_PAGES):
        idx = range(c0, c0 + DEC_PAGES)
        carry_d, acc_d = _decode_chunk(qbd, biasd_ref[...], [kbuf[slot, i] for i in idx],
                                       [vbuf[slot, i] for i in idx], carry_d, acc_d, neg_from_m)
    acc_ref[...] = acc_d
    carry_ref[...] = carry_d

    @pl.when(part == steps_per_sample - 1)
    def _():
        od = jnp.sum(jnp.where(own, acc_d, 0.0), axis=0, keepdims=True)
        od_ref[0] = od * gated_ref[0]

    carries, acc = _sb_block(q2, kmb_ref[...], (vm0_ref[...], vm1_ref[...]),
                             biases, carries, acc, meta_mask, neg_from_m)
    o_ref[0] = acc * gate_ref[0]


def _merge_kernel(x_ref, a_ref, b_ref, w_ref, y_ref):
    half = a_ref.shape[-1]
    y = _dot(a_ref[...].astype(BF16), w_ref[0:half, :])
    y = y + _dot(b_ref[...].astype(BF16), w_ref[half:2 * half, :])
    y_ref[...] = x_ref[...] + y


def _merge(x2d, a, b, w_bf16, tm):
    rows, d = x2d.shape
    half = a.shape[-1]
    return pl.pallas_call(
        _merge_kernel,
        grid=(rows // tm,),
        in_specs=[
            pl.BlockSpec((tm, d), lambda i: (i, 0)),
            pl.BlockSpec((tm, half), lambda i: (i, 0)),
            pl.BlockSpec((tm, half), lambda i: (i, 0)),
            pl.BlockSpec(w_bf16.shape, lambda i: (0, 0)),
        ],
        out_specs=pl.BlockSpec((tm, d), lambda i: (i, 0)),
        out_shape=jax.ShapeDtypeStruct((rows, d), F32),
        compiler_params=pltpu.CompilerParams(
            dimension_semantics=("parallel",), vmem_limit_bytes=VMEM_LIMIT),
        name="merge",
    )(x2d, a, b, w_bf16)


def kernel(x_prompt, x_sample, cache_k, cache_v, state_hgrn, page_table, meta_tokens, norm_g,
           w_in, lb_logits, hg_norm_g, q_norm_g, k_norm_g, sb_bias, w_out):
    nb, seq, d = x_prompt.shape
    db = x_sample.shape[0]
    n_pages = page_table.shape[1]
    n_phys = cache_k.shape[1]
    assert w_in.shape[0] == 1 and w_in.shape[2] == N_SEG * SEG
    assert seq % SB_BLK == 0 and seq % HG_CHUNK == 0
    assert db % STEP_SAMPLES == 0 and (nb * seq) % PROJ_ROWS == 0

    w_in_b = w_in[0].astype(BF16)
    w_out_b = w_out[0].astype(BF16)
    g_row = norm_g[0][None, :]
    qg = jnp.tile(q_norm_g[0], SB_HEADS)[None, :]
    kg = jnp.tile(k_norm_g[0], SB_HEADS)[None, :]
    ng = hg_norm_g[0].reshape(1, HG_HEADS * HG_DIM)
    lane_head = jnp.arange(MXU_TILE) // SB_DH
    grp = (lane_head[:, None] == lane_head[None, :]).astype(BF16)
    bias = sb_bias[0].astype(F32)

    x2d = x_prompt.reshape(nb * seq, d)
    hq, lf, hv, hgate, sq, sk, sv, sgate = _project(x2d, PROJ_ROWS, g_row, w_in_b, lb_logits, qg, kg, grp)
    xs2d = x_sample.reshape(db, d)
    small = jnp.concatenate([xs2d, meta_tokens.astype(F32)], axis=0)
    n_small = small.shape[0]
    s_out = _project(small, n_small, g_row, w_in_b, lb_logits, qg, kg, grp)
    hq_s, lf_s, hv_s, hgate_s, sq_s, sk_s, sv_s, sgate_s = [a[:db] for a in s_out]
    _, lf_m, hv_m, _, _, sk_m, sv_m, _ = [a[db:] for a in s_out]

    pad_m = ((0, HG_CHUNK - N_META), (0, 0))
    st_meta = pl.pallas_call(
        _hgrn_meta_kernel,
        out_shape=jax.ShapeDtypeStruct((HG_HEADS, HG_DIM, HG_DIM), F32),
        name="hgrn_meta",
    )(jnp.pad(lf_m, pad_m), jnp.pad(hv_m, pad_m))

    def r3(a):
        return a.reshape(nb, seq, SEG)

    n_chunks = seq // HG_CHUNK
    cspec = pl.BlockSpec((1, HG_CHUNK, SEG), lambda b, c: (b, c, 0))
    a_hg, s_prompt = pl.pallas_call(
        _hgrn_prompt_kernel,
        grid=(nb, n_chunks),
        in_specs=[cspec, cspec, cspec, cspec,
                  pl.BlockSpec((1, SEG), lambda b, c: (0, 0)),
                  pl.BlockSpec((HG_HEADS, HG_DIM, HG_DIM), lambda b, c: (0, 0, 0))],
        out_specs=[cspec,
                   pl.BlockSpec((1, HG_HEADS, HG_DIM, HG_DIM), lambda b, c: (b, 0, 0, 0))],
        out_shape=[jax.ShapeDtypeStruct((nb, seq, SEG), F32),
                   jax.ShapeDtypeStruct((nb, HG_HEADS, HG_DIM, HG_DIM), F32)],
        scratch_shapes=[pltpu.VMEM((HG_HEADS, HG_DIM, HG_DIM), F32)],
        compiler_params=pltpu.CompilerParams(
            dimension_semantics=("parallel", "arbitrary"), vmem_limit_bytes=VMEM_LIMIT),
        name="hgrn_prompt",
    )(r3(hq), r3(lf), r3(hv), r3(hgate), ng, st_meta)

    rspec = pl.BlockSpec((STEP_SAMPLES, SEG), lambda i: (i, 0))
    sspec = pl.BlockSpec((STEP_SAMPLES, HG_HEADS, HG_DIM, HG_DIM), lambda i: (i, 0, 0, 0))
    s_sample, a_hg_s = pl.pallas_call(
        _hgrn_step_kernel,
        grid=(db // STEP_SAMPLES,),
        in_specs=[rspec, rspec, rspec, rspec, pl.BlockSpec((1, SEG), lambda i: (0, 0)), sspec],
        out_specs=[sspec, rspec],
        out_shape=[jax.ShapeDtypeStruct((db, HG_HEADS, HG_DIM, HG_DIM), F32),
                   jax.ShapeDtypeStruct((db, SEG), F32)],
        compiler_params=pltpu.CompilerParams(
            dimension_semantics=("parallel",), vmem_limit_bytes=VMEM_LIMIT),
        name="hgrn_step",
    )(hq_s, lf_s, hv_s, hgate_s, ng, state_hgrn[0])

    pad_k = ((0, PAGE - N_META), (0, 0))
    n_qb = seq // SB_BLK
    n_pairs = SB_HEADS // 2
    n_steps = nb * n_pairs * n_qb
    assert (db * n_pages) % n_steps == 0
    step_pages = db * n_pages // n_steps
    assert n_pages % step_pages == 0 and step_pages % DEC_PAGES == 0
    steps_per_sample = n_pages // step_pages
    ck = jnp.transpose(cache_k[0], (0, 2, 3, 1)).reshape(n_phys, SEG, PAGE)
    cv = jnp.transpose(cache_v[0], (0, 2, 3, 1)).reshape(n_phys, SEG, PAGE)
    bias_bc = jnp.broadcast_to(bias[:, None], (SB_HEADS, DEC_PAGES * PAGE))

    qspec = pl.BlockSpec((1, SB_BLK, 2 * SB_DH), lambda b, p, i, *_: (b, i, p))
    kvspec = pl.BlockSpec((1, seq, 2 * SB_DH), lambda b, p, i, *_: (b, 0, p))
    mspec = pl.BlockSpec((PAGE, 2 * SB_DH), lambda b, p, i, *_: (0, p))
    vec_spec = pl.BlockSpec(
        (1, 1, SEG), lambda b, p, i, *_: (((b * n_pairs + p) * n_qb + i) // steps_per_sample, 0, 0))
    hbm_spec = pl.BlockSpec(memory_space=pl.ANY)
    b_sb, b_sb_s = pl.pallas_call(
        _sb_kernel,
        grid_spec=pltpu.PrefetchScalarGridSpec(
            num_scalar_prefetch=2,
            grid=(nb, n_pairs, n_qb),
            in_specs=[qspec, kvspec, kvspec, mspec, mspec, qspec,
                      vec_spec, vec_spec,
                      pl.BlockSpec((SB_HEADS, DEC_PAGES * PAGE), lambda b, p, i, *_: (0, 0)),
                      hbm_spec, hbm_spec],
            out_specs=[qspec, vec_spec],
            scratch_shapes=[pltpu.VMEM((seq, 2 * SB_DH), BF16)] * 3
                           + [pltpu.VMEM((PAGE, 2 * SB_DH), BF16)] * 3
                           + [pltpu.VMEM((2, step_pages, SEG, PAGE), F32)] * 2
                           + [pltpu.SemaphoreType.DMA((2, 2)),
                              pltpu.VMEM((SB_HEADS, SEG), F32), pltpu.VMEM((SB_HEADS, PAGE), F32)],
        ),
        out_shape=[jax.ShapeDtypeStruct((nb, seq, SEG), F32),
                   jax.ShapeDtypeStruct((db, 1, SEG), F32)],
        compiler_params=pltpu.CompilerParams(
            dimension_semantics=("arbitrary", "arbitrary", "arbitrary"),
            vmem_limit_bytes=SB_VMEM_LIMIT),
        name="sb_attention",
    )(bias, page_table,
      r3(sq), r3(sk), r3(sv), jnp.pad(sk_m, pad_k), jnp.pad(sv_m, pad_k), r3(sgate),
      sq_s.reshape(db, 1, SEG), sgate_s.reshape(db, 1, SEG), bias_bc, ck, cv)

    y_prompt = _merge(x2d, a_hg.reshape(nb * seq, SEG), b_sb.reshape(nb * seq, SEG), w_out_b, PROJ_ROWS)
    y_sample = _merge(xs2d, a_hg_s, b_sb_s.reshape(db, SEG), w_out_b, db)

    def with_meta(meta_rows, real):
        m = jnp.broadcast_to(meta_rows[None], (nb, N_META, SEG))
        return jnp.concatenate([m, r3(real)], axis=1).reshape(1, nb, N_META + seq, SB_HEADS, SB_DH)

    return (y_prompt.reshape(nb, seq, d),
            y_sample.reshape(db, 1, d),
            with_meta(sk_m, sk),
            with_meta(sv_m, sv),
            s_prompt[None],
            sk_s.reshape(1, db, 1, SB_HEADS, SB_DH),
            sv_s.reshape(1, db, 1, SB_HEADS, SB_DH),
            s_sample[None])
```

```python
import jax
import jax.numpy as jnp
from jax import lax
from jax.experimental import pallas as pl
from jax.experimental.pallas import tpu as pltpu

F32 = jnp.float32
BF16 = jnp.bfloat16
EPS = 1e-6

SEG = 512
N_SEG = 8
HG_HEADS = 4
HG_DIM = 128
SB_HEADS = 8
SB_DH = 64
PAGE = 128
MXU_TILE = 256
N_META = 16

HG_CHUNK = 128
SB_BLK = 256
PROJ_ROWS = 256
DEC_PAGES = 16
STEP_SAMPLES = 8

VMEM_LIMIT = 48 * 1024 * 1024
SB_VMEM_LIMIT = 56 * 1024 * 1024

NT_DIMS = (((1,), (1,)), ((), ()))


def _dot(a, b):
    return jnp.dot(a, b, preferred_element_type=F32)


def _dot_nt(a, b):
    return lax.dot_general(a, b, NT_DIMS, preferred_element_type=F32)


def _split2(x):
    hi = x.astype(BF16)
    lo = (x - hi.astype(F32)).astype(BF16)
    return hi, lo


def _split3(x):
    hi = x.astype(BF16)
    r = x - hi.astype(F32)
    mid = r.astype(BF16)
    lo = (r - mid.astype(F32)).astype(BF16)
    return hi, mid, lo


def _softplus(z):
    return jnp.maximum(z, 0.0) + jnp.log(1.0 + jnp.exp(-jnp.abs(z)))


def _silu(z):
    return z * (1.0 / (1.0 + jnp.exp(-z)))


def _proj_kernel(x_ref, g_ref, w_ref, lbl_ref, qg_ref, kg_ref, grp_ref,
                 hq_ref, lf_ref, hv_ref, hgate_ref, sq_ref, sk_ref, sv_ref, sgate_ref):
    x = x_ref[...]
    ms = jnp.mean(x * x, axis=-1, keepdims=True)
    h = (x * lax.rsqrt(ms + EPS) * g_ref[...]).astype(BF16)

    def seg(j):
        return _dot(h, w_ref[:, j * SEG:(j + 1) * SEG])

    def group_rms(u, gain):
        hi, lo = _split2(u * u)
        gw = grp_ref.shape[0]
        ss = jnp.concatenate(
            [_dot(hi[:, c:c + gw], grp_ref[...]) + _dot(lo[:, c:c + gw], grp_ref[...])
             for c in range(0, SEG, gw)], axis=1)
        return u * lax.rsqrt(ss * (1.0 / SB_DH) + EPS) * gain

    hq_ref[...] = _silu(seg(0))

    l = lbl_ref[...]
    e = jnp.exp(l - jnp.max(l, axis=0, keepdims=True))
    lb = e[0:1, :] / jnp.sum(e, axis=0, keepdims=True)
    a = jnp.log(lb)
    z = seg(1)
    bb = jnp.log1p(-lb) + (jnp.minimum(z, 0.0) - jnp.log1p(jnp.exp(-jnp.abs(z))))
    lf_ref[...] = jnp.maximum(a, bb) + jnp.log1p(jnp.exp(-jnp.abs(a - bb)))

    hv_ref[...] = seg(2)
    hgate_ref[...] = _silu(seg(3))
    sq_ref[...] = group_rms(seg(4), qg_ref[...])
    sk_ref[...] = group_rms(seg(5), kg_ref[...])
    sv_ref[...] = seg(6)
    sgate_ref[...] = _silu(seg(7))


def _project(x2d, tm, norm_g, w_bf16, lb_logits, qg, kg, grp):
    rows, d = x2d.shape
    const = lambda i: (0, 0)
    row_spec = pl.BlockSpec((tm, SEG), lambda i: (i, 0))
    return pl.pallas_call(
        _proj_kernel,
        grid=(rows // tm,),
        in_specs=[
            pl.BlockSpec((tm, d), lambda i: (i, 0)),
            pl.BlockSpec((1, d), const),
            pl.BlockSpec(w_bf16.shape, const),
            pl.BlockSpec(lb_logits.shape, const),
            pl.BlockSpec((1, SEG), const),
            pl.BlockSpec((1, SEG), const),
            pl.BlockSpec(grp.shape, const),
        ],
        out_specs=[row_spec] * N_SEG,
        out_shape=[jax.ShapeDtypeStruct((rows, SEG), F32)] * N_SEG,
        compiler_params=pltpu.CompilerParams(
            dimension_semantics=("parallel",), vmem_limit_bytes=VMEM_LIMIT),
        name="proj",
    )(x2d, norm_g, w_bf16, lb_logits, qg, kg, grp)


def _hgrn_chunk(q, lf, v, st, want_out):
    c = q.shape[0]
    width = q.shape[1]
    r = lax.broadcasted_iota(jnp.int32, (c, c), 0)
    s = lax.broadcasted_iota(jnp.int32, (c, c), 1)
    tri = (s <= r).astype(BF16)
    b = sum(_dot(tri, p) for p in _split3(lf))
    kk = 1.0 - jnp.exp(lf)
    b_last = b[c - 1:c, :]
    kdec = (kk * jnp.exp(jnp.minimum(b_last - b, 0.0))).astype(BF16)
    sdec = jnp.exp(b_last)

    new_st = []
    for h in range(HG_HEADS):
        sl = slice(h * HG_DIM, (h + 1) * HG_DIM)
        vt = v[:, sl].T.astype(BF16)
        new_st.append(st[h] * sdec[:, sl] + _dot(vt, kdec[:, sl]))
    if not want_out:
        return new_st, None

    levels = [(0, q, kk)]
    m = c // 2
    while m >= 8:
        nb = c // m
        starts = [b[i * m:i * m + 1, :] for i in range(nb)]
        hold = jnp.concatenate([jnp.broadcast_to(starts[i], (m, width)) for i in range(nb)], axis=0)
        nxt = jnp.concatenate(
            [jnp.broadcast_to(starts[min(i + 1, nb - 1)], (m, width)) for i in range(nb)], axis=0)
        levels.append((m, q * jnp.exp(jnp.minimum(b - hold, 0.0)),
                       kk * jnp.exp(jnp.minimum(nxt - b, 0.0))))
        m //= 2
    ri = lax.broadcasted_iota(jnp.int32, (c, width), 0)
    r4 = ri & 3
    r2 = ri & 1
    dn1 = pltpu.roll(lf, 1, 0)
    dn2 = pltpu.roll(lf, 2, 0)
    up1 = pltpu.roll(lf, c - 1, 0)
    up2 = pltpu.roll(lf, c - 2, 0)
    up3 = pltpu.roll(lf, c - 3, 0)
    up4 = pltpu.roll(lf, c - 4, 0)
    zero = jnp.zeros_like(lf)
    eq4 = jnp.where(r4 > 0, lf, zero) + jnp.where(r4 > 1, dn1, zero) + jnp.where(r4 > 2, dn2, zero)
    ek4 = up1 + jnp.where(r4 <= 2, up2, zero) + jnp.where(r4 <= 1, up3, zero) + jnp.where(r4 == 0, up4, zero)
    eq2 = jnp.where(r2 > 0, lf, zero)
    ek2 = up1 + jnp.where(r2 == 0, up2, zero)
    levels.append((4, q * jnp.exp(eq4), kk * jnp.exp(ek4)))
    levels.append((2, q * jnp.exp(eq2), kk * jnp.exp(ek2)))
    levels.append((1, q, kk * jnp.exp(up1)))

    x = r ^ s
    lower = s < r
    masks = [r == s if m == 0 else lower & (x >= m) & (x < 2 * m) for m, _, _ in levels]
    qdec = (q * jnp.exp(b)).astype(BF16)
    levels = [(qm.astype(BF16), km.astype(BF16)) for _, qm, km in levels]

    outs = []
    for h in range(HG_HEADS):
        sl = slice(h * HG_DIM, (h + 1) * HG_DIM)
        a = jnp.zeros((c, c), F32)
        for mask, (qm, km) in zip(masks, levels):
            a = jnp.where(mask, _dot_nt(qm[:, sl], km[:, sl]), a)
        o = _dot_nt(qdec[:, sl], st[h].astype(BF16)) + _dot(a.astype(BF16), v[:, sl].astype(BF16))
        outs.append(o)
    return new_st, outs


def _hgrn_meta_kernel(lf_ref, v_ref, st_ref):
    st0 = [jnp.zeros((HG_DIM, HG_DIM), F32)] * HG_HEADS
    lf = lf_ref[...]
    new_st, _ = _hgrn_chunk(lf, lf, v_ref[...], st0, False)
    for h in range(HG_HEADS):
        st_ref[h] = new_st[h]


def _hgrn_prompt_kernel(q_ref, lf_ref, v_ref, gate_ref, ng_ref, st0_ref, x_ref, sb_ref, w_ref,
                        y_ref, s_ref, st_ref):
    ci = pl.program_id(1)

    @pl.when(ci == 0)
    def _():
        st_ref[...] = st0_ref[...]

    st = [st_ref[h] for h in range(HG_HEADS)]
    new_st, outs = _hgrn_chunk(q_ref[0], lf_ref[0], v_ref[0], st, True)
    gated = []
    for h in range(HG_HEADS):
        sl = slice(h * HG_DIM, (h + 1) * HG_DIM)
        st_ref[h] = new_st[h]
        o = outs[h]
        ms = jnp.mean(o * o, axis=-1, keepdims=True)
        gated.append((o * lax.rsqrt(ms + EPS) * ng_ref[:, sl] * gate_ref[0, :, sl]).astype(BF16))
    a = jnp.concatenate(gated, axis=1)
    half = a.shape[1]
    y = _dot(a, w_ref[0:half, :]) + _dot(sb_ref[0].astype(BF16), w_ref[half:2 * half, :])
    y_ref[0] = x_ref[0] + y

    @pl.when(ci == pl.num_programs(1) - 1)
    def _():
        for h in range(HG_HEADS):
            s_ref[0, h] = new_st[h].T


def _hgrn_step_kernel(q_ref, lf_ref, v_ref, gate_ref, ng_ref, s0_ref, s_ref, a_ref):
    for i in range(STEP_SAMPLES):
        q = q_ref[i:i + 1, :]
        f = jnp.exp(lf_ref[i:i + 1, :])
        v = v_ref[i:i + 1, :]
        qk = q * (1.0 - f)
        for h in range(HG_HEADS):
            sl = slice(h * HG_DIM, (h + 1) * HG_DIM)
            fc = jnp.broadcast_to(f[:, sl], (HG_DIM, HG_DIM)).T
            sn = fc * s0_ref[i, h] + (1.0 - fc) * v[:, sl]
            s_ref[i, h] = sn
            qf = jnp.broadcast_to(q[:, sl] * f[:, sl], (8, HG_DIM)).astype(BF16)
            o = _dot(qf, s0_ref[i, h].astype(BF16))[0:1, :]
            o = o + jnp.sum(qk[:, sl], axis=-1, keepdims=True) * v[:, sl]
            ms = jnp.mean(o * o, axis=-1, keepdims=True)
            a_ref[i:i + 1, sl] = o * lax.rsqrt(ms + EPS) * ng_ref[:, sl] * gate_ref[i:i + 1, sl]


def _sb_block(q2, kb, vbs, biases, carries, acc, mask, neg_from):
    tq = q2.shape[0] // 2
    z2 = _dot_nt(q2, kb)
    new_carries = []
    for h in range(2):
        z = z2[h * tq:(h + 1) * tq] + biases[h]
        sp = _softplus(z)
        if mask is not None:
            sp = jnp.where(mask, sp, 0.0)
        hi, lo = _split2(sp)
        tail = _dot(hi, neg_from) + _dot(lo, neg_from)
        a = jnp.exp(z + tail + carries[h])
        if mask is not None:
            a = jnp.where(mask, a, 0.0)
        acc = acc + _dot(a.astype(BF16), vbs[h])
        new_carries.append(carries[h] - jnp.sum(sp, axis=-1, keepdims=True))
    return tuple(new_carries), acc


def _decode_chunk(qbd, bias, k_pages, v_pages, carry, acc, neg_from):
    n = len(k_pages)

    def lanes(x, i):
        return x[:, i * PAGE:(i + 1) * PAGE]

    k_all = jnp.concatenate([kp[...].astype(BF16) for kp in k_pages], axis=1)
    z = _dot(qbd, k_all) + bias
    sp = _softplus(z)
    sp_rows = jnp.concatenate([lanes(sp, i) for i in range(n)], axis=0)
    hi, lo = _split2(sp_rows)
    tail = _dot(hi, neg_from) + _dot(lo, neg_from)
    tot = jnp.sum(sp_rows, axis=-1, keepdims=True)
    weights = []
    for i in range(n):
        rows = slice(i * SB_HEADS, (i + 1) * SB_HEADS)
        weights.append(jnp.exp(lanes(z, i) + tail[rows] + carry))
        carry = carry - tot[rows]
    new_acc = []
    for h in range(SB_HEADS):
        rows = slice(h * SB_DH, (h + 1) * SB_DH)
        t = acc[h]
        for i in range(n):
            t = t + v_pages[i][rows, :] * jnp.broadcast_to(weights[i][h:h + 1, :], (SB_DH, PAGE))
        new_acc.append(t)
    return carry, new_acc


def _sb_kernel(bias_ref, pt_ref,
               q_ref, k_ref, v_ref, km_ref, vm_ref, gate_ref,
               qd_ref, gated_ref, biasd_ref, ck_ref, cv_ref,
               o_ref, od_ref,
               kb_ref, v0_ref, v1_ref, kmb_ref, vm0_ref, vm1_ref,
               kbuf, vbuf, sem, acc_ref, carry_ref):
    p = pl.program_id(1)
    qi = pl.program_id(2)
    step = (pl.program_id(0) * pl.num_programs(1) + p) * pl.num_programs(2) + qi
    n_steps = pl.num_programs(0) * pl.num_programs(1) * pl.num_programs(2)
    step_pages = kbuf.shape[1]
    n_pages = pt_ref.shape[1]
    steps_per_sample = n_pages // step_pages

    def page_copies(st, slot):
        sample = st // steps_per_sample
        first = (st % steps_per_sample) * step_pages
        copies = []
        for i in range(step_pages):
            page = pt_ref[sample, n_pages - 1 - (first + i)]
            copies.append(pltpu.make_async_copy(ck_ref.at[page], kbuf.at[slot, i], sem.at[0, slot]))
            copies.append(pltpu.make_async_copy(cv_ref.at[page], vbuf.at[slot, i], sem.at[1, slot]))
        return copies

    slot = step % 2

    @pl.when(step == 0)
    def _():
        for c in page_copies(step, slot):
            c.start()

    lane = lax.broadcasted_iota(jnp.int32, (1, 2 * SB_DH), 1)
    head0 = lane < SB_DH

    @pl.when(qi == 0)
    def _():
        kb_ref[...] = k_ref[0].astype(BF16)
        v = v_ref[0]
        v0_ref[...] = jnp.where(head0, v, 0.0).astype(BF16)
        v1_ref[...] = jnp.where(head0, 0.0, v).astype(BF16)
        kmb_ref[...] = km_ref[...].astype(BF16)
        vm = vm_ref[...]
        vm0_ref[...] = jnp.where(head0, vm, 0.0).astype(BF16)
        vm1_ref[...] = jnp.where(head0, 0.0, vm).astype(BF16)

    q = q_ref[0] * (SB_DH ** -0.5)
    q2 = jnp.concatenate([jnp.where(head0, q, 0.0), jnp.where(head0, 0.0, q)], axis=0).astype(BF16)
    biases = (bias_ref[2 * p], bias_ref[2 * p + 1])
    r = lax.broadcasted_iota(jnp.int32, (SB_BLK, SB_BLK), 0)
    s = lax.broadcasted_iota(jnp.int32, (SB_BLK, SB_BLK), 1)
    neg_from = jnp.where(r >= s, -1.0, 0.0).astype(BF16)
    diag_mask = s < r
    rm = lax.broadcasted_iota(jnp.int32, (PAGE, PAGE), 0)
    sm = lax.broadcasted_iota(jnp.int32, (PAGE, PAGE), 1)
    neg_from_m = jnp.where(rm >= sm, -1.0, 0.0).astype(BF16)
    meta_mask = lax.broadcasted_iota(jnp.int32, (SB_BLK, PAGE), 1) < N_META

    def block_at(start, carries, acc, mask):
        rows = pl.ds(start, SB_BLK)
        return _sb_block(q2, kb_ref[rows, :], (v0_ref[rows, :], v1_ref[rows, :]),
                         biases, carries, acc, mask, neg_from)

    ahead = jnp.where(step + 1 < n_steps, step + 1, 0)
    for c in page_copies(ahead, 1 - slot):
        c.start()

    zero_c = jnp.zeros((SB_BLK, 1), F32)
    carries, acc = block_at(pl.multiple_of(qi * SB_BLK, SB_BLK), (zero_c, zero_c),
                            jnp.zeros((SB_BLK, 2 * SB_DH), F32), diag_mask)

    def body(n, state):
        return block_at(pl.multiple_of((qi - 1 - n) * SB_BLK, SB_BLK), state[0], state[1], None)

    carries, acc = lax.fori_loop(0, qi, body, (carries, acc))

    for c in page_copies(step, slot):
        c.wait()
    part = step % steps_per_sample

    @pl.when(part == 0)
    def _():
        acc_ref[...] = jnp.zeros_like(acc_ref)
        carry_ref[...] = jnp.zeros_like(carry_ref)

    width = SB_HEADS * SB_DH
    hrow = lax.broadcasted_iota(jnp.int32, (SB_HEADS, width), 0)
    hlane = lax.broadcasted_iota(jnp.int32, (SB_HEADS, width), 1) // SB_DH
    own = hrow == hlane
    qd = jnp.broadcast_to(qd_ref[0] * (SB_DH ** -0.5), (SB_HEADS, width))
    qbd = jnp.where(own, qd, 0.0).astype(BF16)
    carry_d = carry_ref[...]
    acc_d = [acc_ref[h * SB_DH:(h + 1) * SB_DH, :] for h in range(SB_HEADS)]
    for c0 in range(0, step_pages, DEC_PAGES):
        idx = range(c0, c0 + DEC_PAGES)
        carry_d, acc_d = _decode_chunk(qbd, biasd_ref[...], [kbuf.at[slot, i] for i in idx],
                                       [vbuf.at[slot, i] for i in idx], carry_d, acc_d, neg_from_m)
    for h in range(SB_HEADS):
        acc_ref[h * SB_DH:(h + 1) * SB_DH, :] = acc_d[h]
    carry_ref[...] = carry_d

    @pl.when(part == steps_per_sample - 1)
    def _():
        od = jnp.sum(acc_ref[...].T, axis=0, keepdims=True)
        od_ref[0] = od * gated_ref[0]

    carries, acc = _sb_block(q2, kmb_ref[...], (vm0_ref[...], vm1_ref[...]),
                             biases, carries, acc, meta_mask, neg_from_m)
    o_ref[0] = acc * gate_ref[0]

    @pl.when(step == n_steps - 1)
    def _():
        for c in page_copies(ahead, 1 - slot):
            c.wait()


def _merge_kernel(x_ref, a_ref, b_ref, w_ref, y_ref):
    half = a_ref.shape[-1]
    y = _dot(a_ref[...].astype(BF16), w_ref[0:half, :])
    y = y + _dot(b_ref[...].astype(BF16), w_ref[half:2 * half, :])
    y_ref[...] = x_ref[...] + y


def _merge(x2d, a, b, w_bf16, tm):
    rows, d = x2d.shape
    half = a.shape[-1]
    return pl.pallas_call(
        _merge_kernel,
        grid=(rows // tm,),
        in_specs=[
            pl.BlockSpec((tm, d), lambda i: (i, 0)),
            pl.BlockSpec((tm, half), lambda i: (i, 0)),
            pl.BlockSpec((tm, half), lambda i: (i, 0)),
            pl.BlockSpec(w_bf16.shape, lambda i: (0, 0)),
        ],
        out_specs=pl.BlockSpec((tm, d), lambda i: (i, 0)),
        out_shape=jax.ShapeDtypeStruct((rows, d), F32),
        compiler_params=pltpu.CompilerParams(
            dimension_semantics=("parallel",), vmem_limit_bytes=VMEM_LIMIT),
        name="merge",
    )(x2d, a, b, w_bf16)


def kernel(x_prompt, x_sample, cache_k, cache_v, state_hgrn, page_table, meta_tokens, norm_g,
           w_in, lb_logits, hg_norm_g, q_norm_g, k_norm_g, sb_bias, w_out):
    nb, seq, d = x_prompt.shape
    db = x_sample.shape[0]
    n_pages = page_table.shape[1]
    n_phys = cache_k.shape[1]
    assert w_in.shape[0] == 1 and w_in.shape[2] == N_SEG * SEG
    assert seq % SB_BLK == 0 and seq % HG_CHUNK == 0
    assert db % STEP_SAMPLES == 0 and (nb * seq) % PROJ_ROWS == 0

    w_in_b = w_in[0].astype(BF16)
    w_out_b = w_out[0].astype(BF16)
    g_row = norm_g[0][None, :]
    qg = jnp.tile(q_norm_g[0], SB_HEADS)[None, :]
    kg = jnp.tile(k_norm_g[0], SB_HEADS)[None, :]
    ng = hg_norm_g[0].reshape(1, HG_HEADS * HG_DIM)
    lane_head = jnp.arange(MXU_TILE) // SB_DH
    grp = (lane_head[:, None] == lane_head[None, :]).astype(BF16)
    bias = sb_bias[0].astype(F32)

    x2d = x_prompt.reshape(nb * seq, d)
    hq, lf, hv, hgate, sq, sk, sv, sgate = _project(x2d, PROJ_ROWS, g_row, w_in_b, lb_logits, qg, kg, grp)
    xs2d = x_sample.reshape(db, d)
    small = jnp.concatenate([xs2d, meta_tokens.astype(F32)], axis=0)
    n_small = small.shape[0]
    s_out = _project(small, n_small, g_row, w_in_b, lb_logits, qg, kg, grp)
    hq_s, lf_s, hv_s, hgate_s, sq_s, sk_s, sv_s, sgate_s = [a[:db] for a in s_out]
    _, lf_m, hv_m, _, _, sk_m, sv_m, _ = [a[db:] for a in s_out]

    pad_m = ((0, HG_CHUNK - N_META), (0, 0))
    st_meta = pl.pallas_call(
        _hgrn_meta_kernel,
        out_shape=jax.ShapeDtypeStruct((HG_HEADS, HG_DIM, HG_DIM), F32),
        name="hgrn_meta",
    )(jnp.pad(lf_m, pad_m), jnp.pad(hv_m, pad_m))

    def r3(a):
        return a.reshape(nb, seq, SEG)

    rspec = pl.BlockSpec((STEP_SAMPLES, SEG), lambda i: (i, 0))
    sspec = pl.BlockSpec((STEP_SAMPLES, HG_HEADS, HG_DIM, HG_DIM), lambda i: (i, 0, 0, 0))
    s_sample, a_hg_s = pl.pallas_call(
        _hgrn_step_kernel,
        grid=(db // STEP_SAMPLES,),
        in_specs=[rspec, rspec, rspec, rspec, pl.BlockSpec((1, SEG), lambda i: (0, 0)), sspec],
        out_specs=[sspec, rspec],
        out_shape=[jax.ShapeDtypeStruct((db, HG_HEADS, HG_DIM, HG_DIM), F32),
                   jax.ShapeDtypeStruct((db, SEG), F32)],
        compiler_params=pltpu.CompilerParams(
            dimension_semantics=("parallel",), vmem_limit_bytes=VMEM_LIMIT),
        name="hgrn_step",
    )(hq_s, lf_s, hv_s, hgate_s, ng, state_hgrn[0])

    pad_k = ((0, PAGE - N_META), (0, 0))
    n_qb = seq // SB_BLK
    n_pairs = SB_HEADS // 2
    n_steps = nb * n_pairs * n_qb
    assert (db * n_pages) % n_steps == 0
    step_pages = db * n_pages // n_steps
    assert n_pages % step_pages == 0 and step_pages % DEC_PAGES == 0
    steps_per_sample = n_pages // step_pages
    ck = jnp.transpose(cache_k[0], (0, 2, 3, 1)).reshape(n_phys, SEG, PAGE)
    cv = jnp.transpose(cache_v[0], (0, 2, 3, 1)).reshape(n_phys, SEG, PAGE)
    bias_bc = jnp.broadcast_to(bias[:, None], (SB_HEADS, DEC_PAGES * PAGE))

    qspec = pl.BlockSpec((1, SB_BLK, 2 * SB_DH), lambda b, p, i, *_: (b, i, p))
    kvspec = pl.BlockSpec((1, seq, 2 * SB_DH), lambda b, p, i, *_: (b, 0, p))
    mspec = pl.BlockSpec((PAGE, 2 * SB_DH), lambda b, p, i, *_: (0, p))
    vec_spec = pl.BlockSpec(
        (1, 1, SEG), lambda b, p, i, *_: (((b * n_pairs + p) * n_qb + i) // steps_per_sample, 0, 0))
    hbm_spec = pl.BlockSpec(memory_space=pl.ANY)
    b_sb, b_sb_s = pl.pallas_call(
        _sb_kernel,
        grid_spec=pltpu.PrefetchScalarGridSpec(
            num_scalar_prefetch=2,
            grid=(nb, n_pairs, n_qb),
            in_specs=[qspec, kvspec, kvspec, mspec, mspec, qspec,
                      vec_spec, vec_spec,
                      pl.BlockSpec((SB_HEADS, DEC_PAGES * PAGE), lambda b, p, i, *_: (0, 0)),
                      hbm_spec, hbm_spec],
            out_specs=[qspec, vec_spec],
            scratch_shapes=[pltpu.VMEM((seq, 2 * SB_DH), BF16)] * 3
                           + [pltpu.VMEM((PAGE, 2 * SB_DH), BF16)] * 3
                           + [pltpu.VMEM((2, step_pages, SEG, PAGE), F32)] * 2
                           + [pltpu.SemaphoreType.DMA((2, 2)),
                              pltpu.VMEM((SEG, PAGE), F32), pltpu.VMEM((SB_HEADS, PAGE), F32)],
        ),
        out_shape=[jax.ShapeDtypeStruct((nb, seq, SEG), F32),
                   jax.ShapeDtypeStruct((db, 1, SEG), F32)],
        compiler_params=pltpu.CompilerParams(
            dimension_semantics=("arbitrary", "arbitrary", "arbitrary"),
            vmem_limit_bytes=SB_VMEM_LIMIT),
        name="sb_attention",
    )(bias, page_table,
      r3(sq), r3(sk), r3(sv), jnp.pad(sk_m, pad_k), jnp.pad(sv_m, pad_k), r3(sgate),
      sq_s.reshape(db, 1, SEG), sgate_s.reshape(db, 1, SEG), bias_bc, ck, cv)

    n_chunks = seq // HG_CHUNK
    cspec = pl.BlockSpec((1, HG_CHUNK, SEG), lambda b, c: (b, c, 0))
    xspec = pl.BlockSpec((1, HG_CHUNK, d), lambda b, c: (b, c, 0))
    y_prompt, s_prompt = pl.pallas_call(
        _hgrn_prompt_kernel,
        grid=(nb, n_chunks),
        in_specs=[cspec, cspec, cspec, cspec,
                  pl.BlockSpec((1, SEG), lambda b, c: (0, 0)),
                  pl.BlockSpec((HG_HEADS, HG_DIM, HG_DIM), lambda b, c: (0, 0, 0)),
                  xspec, cspec,
                  pl.BlockSpec(w_out_b.shape, lambda b, c: (0, 0))],
        out_specs=[xspec,
                   pl.BlockSpec((1, HG_HEADS, HG_DIM, HG_DIM), lambda b, c: (b, 0, 0, 0))],
        out_shape=[jax.ShapeDtypeStruct((nb, seq, d), F32),
                   jax.ShapeDtypeStruct((nb, HG_HEADS, HG_DIM, HG_DIM), F32)],
        scratch_shapes=[pltpu.VMEM((HG_HEADS, HG_DIM, HG_DIM), F32)],
        compiler_params=pltpu.CompilerParams(
            dimension_semantics=("parallel", "arbitrary"), vmem_limit_bytes=VMEM_LIMIT),
        name="hgrn_prompt",
    )(r3(hq), r3(lf), r3(hv), r3(hgate), ng, st_meta, x_prompt, b_sb, w_out_b)

    y_sample = _merge(xs2d, a_hg_s, b_sb_s.reshape(db, SEG), w_out_b, db)

    def with_meta(meta_rows, real):
        m = jnp.broadcast_to(meta_rows[None], (nb, N_META, SEG))
        return jnp.concatenate([m, r3(real)], axis=1).reshape(1, nb, N_META + seq, SB_HEADS, SB_DH)

    return (y_prompt,
            y_sample.reshape(db, 1, d),
            with_meta(sk_m, sk),
            with_meta(sv_m, sv),
            s_prompt[None],
            sk_s.reshape(1, db, 1, SB_HEADS, SB_DH),
            sv_s.reshape(1, db, 1, SB_HEADS, SB_DH),
            s_sample[None])
```

```python
import jax
import jax.numpy as jnp
import numpy as np
from jax import lax
from jax.experimental import pallas as pl
from jax.experimental.pallas import tpu as pltpu

F32 = jnp.float32
BF16 = jnp.bfloat16
EPS = 1e-6

SEG = 512
N_SEG = 8
HG_HEADS = 4
HG_DIM = 128
SB_HEADS = 8
SB_DH = 64
PAGE = 128
MXU_TILE = 256
N_META = 16

HG_CHUNK = 128
HG_SMALL_LEVELS = (4, 2, 1)
HG_BATCH = 2
SB_BLK = 256
PROJ_ROWS = 512
DEC_PAGES = 16
STEP_SAMPLES = 8

VMEM_LIMIT = 56 * 1024 * 1024
SB_VMEM_LIMIT = 56 * 1024 * 1024

NT_DIMS = (((1,), (1,)), ((), ()))


def _dot(a, b):
    return jnp.dot(a, b, preferred_element_type=F32)


def _dot_nt(a, b):
    return lax.dot_general(a, b, NT_DIMS, preferred_element_type=F32)


def _split2(x):
    hi = x.astype(BF16)
    lo = (x - hi.astype(F32)).astype(BF16)
    return hi, lo


def _split3(x):
    hi = x.astype(BF16)
    r = x - hi.astype(F32)
    mid = r.astype(BF16)
    lo = (r - mid.astype(F32)).astype(BF16)
    return hi, mid, lo


def _softplus(z):
    return jnp.maximum(z, 0.0) + jnp.log(1.0 + jnp.exp(-jnp.abs(z)))


def _silu(z):
    return z * (1.0 / (1.0 + jnp.exp(-z)))


def _proj_kernel(x_ref, g_ref, w_ref, lbl_ref, qg_ref, kg_ref, grp_ref,
                 hq_ref, lf_ref, hv_ref, hgate_ref, sq_ref, sk_ref, sv_ref, sgate_ref):
    x = x_ref[...]
    ms = jnp.mean(x * x, axis=-1, keepdims=True)
    h = (x * lax.rsqrt(ms + EPS) * g_ref[...]).astype(BF16)

    def seg(j):
        return _dot(h, w_ref[:, j * SEG:(j + 1) * SEG])

    def group_rms(u, gain):
        hi, lo = _split2(u * u)
        gw = grp_ref.shape[0]
        ss = jnp.concatenate(
            [_dot(hi[:, c:c + gw], grp_ref[...]) + _dot(lo[:, c:c + gw], grp_ref[...])
             for c in range(0, SEG, gw)], axis=1)
        return u * lax.rsqrt(ss * (1.0 / SB_DH) + EPS) * gain

    hq_ref[...] = _silu(seg(0))

    l = lbl_ref[...]
    e = jnp.exp(l - jnp.max(l, axis=0, keepdims=True))
    lb = e[0:1, :] / jnp.sum(e, axis=0, keepdims=True)
    a = jnp.log(lb)
    z = seg(1)
    bb = jnp.log1p(-lb) + (jnp.minimum(z, 0.0) - jnp.log(1.0 + jnp.exp(-jnp.abs(z))))
    lf_ref[...] = jnp.maximum(a, bb) + jnp.log(1.0 + jnp.exp(-jnp.abs(a - bb)))

    hv_ref[...] = seg(2)
    hgate_ref[...] = _silu(seg(3))
    sq_ref[...] = group_rms(seg(4), qg_ref[...])
    sk_ref[...] = group_rms(seg(5), kg_ref[...])
    sv_ref[...] = seg(6)
    sgate_ref[...] = _silu(seg(7))


def _project(x2d, tm, norm_g, w_bf16, lb_logits, qg, kg, grp):
    rows, d = x2d.shape
    const = lambda i: (0, 0)
    row_spec = pl.BlockSpec((tm, SEG), lambda i: (i, 0))
    return pl.pallas_call(
        _proj_kernel,
        grid=(rows // tm,),
        in_specs=[
            pl.BlockSpec((tm, d), lambda i: (i, 0)),
            pl.BlockSpec((1, d), const),
            pl.BlockSpec(w_bf16.shape, const),
            pl.BlockSpec(lb_logits.shape, const),
            pl.BlockSpec((1, SEG), const),
            pl.BlockSpec((1, SEG), const),
            pl.BlockSpec(grp.shape, const),
        ],
        out_specs=[row_spec] * N_SEG,
        out_shape=[jax.ShapeDtypeStruct((rows, SEG), F32)] * N_SEG,
        compiler_params=pltpu.CompilerParams(
            dimension_semantics=("parallel",), vmem_limit_bytes=VMEM_LIMIT),
        name="proj",
    )(x2d, norm_g, w_bf16, lb_logits, qg, kg, grp)


def _hgrn_sum_matrix(c, with_small_levels):
    t = np.arange(c)[:, None]
    j = np.arange(c)[None, :]
    blocks = [j <= t]
    if with_small_levels:
        for m in HG_SMALL_LEVELS[:-1]:
            f = (t // m) * m
            blocks.append((j > f) & (j <= t))
            blocks.append((j > t) & (j <= f + m))
        blocks.append(j == t + 1)
    return jnp.asarray(np.concatenate(blocks, axis=0), dtype=BF16)


def _hgrn_level_index(c):
    t = np.arange(c)[:, None]
    s = np.arange(c)[None, :]
    x = t ^ s
    lvl = np.where(x > 0, 2 ** np.floor(np.log2(np.maximum(x, 1))).astype(np.int64), 0)
    return jnp.asarray(np.where(s > t, -1, lvl), dtype=jnp.int32)


def _hgrn_chunk(q, lf, v, st, sums, lvl):
    c = q.shape[0]
    width = q.shape[1]
    want_out = lvl is not None
    lf_hi, lf_lo = _split2(lf)
    summed = _dot(sums, lf_hi) + _dot(sums, lf_lo)

    def block(i):
        return summed[i * c:(i + 1) * c, :]

    b = block(0)
    kk = 1.0 - jnp.exp(lf)
    b_last = b[c - 1:c, :]
    kdec = (kk * jnp.exp(jnp.minimum(b_last - b, 0.0))).astype(BF16)
    sdec = jnp.exp(b_last)

    new_st = []
    for h in range(HG_HEADS):
        sl = slice(h * HG_DIM, (h + 1) * HG_DIM)
        vt = v[:, sl].T.astype(BF16)
        new_st.append(st[h] * sdec[:, sl] + _dot(vt, kdec[:, sl]))
    if not want_out:
        return new_st, None

    levels = [(0, q, kk)]
    m = c // 2
    while m > HG_SMALL_LEVELS[0]:
        nb = c // m
        starts = [b[i * m:i * m + 1, :] for i in range(nb)]
        hold = jnp.concatenate([jnp.broadcast_to(starts[i], (m, width)) for i in range(nb)], axis=0)
        nxt = jnp.concatenate(
            [jnp.broadcast_to(starts[min(i + 1, nb - 1)], (m, width)) for i in range(nb)], axis=0)
        levels.append((m, q * jnp.exp(b - hold), kk * jnp.exp(jnp.minimum(nxt - b, 0.0))))
        m //= 2
    for i, m in enumerate(HG_SMALL_LEVELS[:-1]):
        levels.append((m, q * jnp.exp(block(1 + 2 * i)), kk * jnp.exp(block(2 + 2 * i))))
    levels.append((1, q, kk * jnp.exp(block(2 * len(HG_SMALL_LEVELS) - 1))))

    masks = [lvl == m for m, _, _ in levels]
    qdec = (q * jnp.exp(b)).astype(BF16)
    levels = [(qm.astype(BF16), km.astype(BF16)) for _, qm, km in levels]

    outs = []
    for h in range(HG_HEADS):
        sl = slice(h * HG_DIM, (h + 1) * HG_DIM)
        a = jnp.zeros((c, c), F32)
        for mask, (qm, km) in zip(masks, levels):
            a = jnp.where(mask, _dot_nt(qm[:, sl], km[:, sl]), a)
        o = _dot_nt(qdec[:, sl], st[h].astype(BF16)) + _dot(a.astype(BF16), v[:, sl].astype(BF16))
        outs.append(o)
    return new_st, outs


def _hgrn_meta_kernel(lf_ref, v_ref, sums_ref, st_ref):
    st0 = [jnp.zeros((HG_DIM, HG_DIM), F32)] * HG_HEADS
    lf = lf_ref[...]
    new_st, _ = _hgrn_chunk(lf, lf, v_ref[...], st0, sums_ref[...], None)
    for h in range(HG_HEADS):
        st_ref[h] = new_st[h]


def _hgrn_prompt_kernel(q_ref, lf_ref, v_ref, gate_ref, ng_ref, st0_ref, sums_ref, lvl_ref,
                        x_ref, sb_ref, w_ref, y_ref, s_ref, st_ref):
    ci = pl.program_id(1)
    n_batch = q_ref.shape[0]

    @pl.when(ci == 0)
    def _():
        for i in range(n_batch):
            st_ref[i] = st0_ref[...]

    final = []
    for i in range(n_batch):
        st = [st_ref[i, h] for h in range(HG_HEADS)]
        new_st, outs = _hgrn_chunk(q_ref[i], lf_ref[i], v_ref[i], st, sums_ref[...], lvl_ref[...])
        final.append(new_st)
        gated = []
        for h in range(HG_HEADS):
            sl = slice(h * HG_DIM, (h + 1) * HG_DIM)
            st_ref[i, h] = new_st[h]
            o = outs[h]
            ms = jnp.mean(o * o, axis=-1, keepdims=True)
            gated.append((o * lax.rsqrt(ms + EPS) * ng_ref[:, sl] * gate_ref[i, :, sl]).astype(BF16))
        a = jnp.concatenate(gated, axis=1)
        half = a.shape[1]
        y = _dot(a, w_ref[0:half, :]) + _dot(sb_ref[i].astype(BF16), w_ref[half:2 * half, :])
        y_ref[i] = x_ref[i] + y

    @pl.when(ci == pl.num_programs(1) - 1)
    def _():
        for i in range(n_batch):
            for h in range(HG_HEADS):
                s_ref[i, h] = final[i][h].T


def _hgrn_step_kernel(q_ref, lf_ref, v_ref, gate_ref, ng_ref, s0_ref, s_ref, a_ref):
    for i in range(STEP_SAMPLES):
        q = q_ref[i:i + 1, :]
        f = jnp.exp(lf_ref[i:i + 1, :])
        v = v_ref[i:i + 1, :]
        qk = q * (1.0 - f)
        for h in range(HG_HEADS):
            sl = slice(h * HG_DIM, (h + 1) * HG_DIM)
            fc = jnp.broadcast_to(f[:, sl], (HG_DIM, HG_DIM)).T
            sn = fc * s0_ref[i, h] + (1.0 - fc) * v[:, sl]
            s_ref[i, h] = sn
            qf = jnp.broadcast_to(q[:, sl] * f[:, sl], (8, HG_DIM)).astype(BF16)
            o = _dot(qf, s0_ref[i, h].astype(BF16))[0:1, :]
            o = o + jnp.sum(qk[:, sl], axis=-1, keepdims=True) * v[:, sl]
            ms = jnp.mean(o * o, axis=-1, keepdims=True)
            a_ref[i:i + 1, sl] = o * lax.rsqrt(ms + EPS) * ng_ref[:, sl] * gate_ref[i:i + 1, sl]


def _sb_weights(z2, biases, carries, mask, neg_from):
    tq = z2.shape[0] // 2
    new_carries, weights = [], []
    for h in range(2):
        z = z2[h * tq:(h + 1) * tq] + biases[h]
        sp = _softplus(z)
        if mask is not None:
            sp = jnp.where(mask, sp, 0.0)
        tail = _dot(sp.astype(BF16), neg_from)
        a = jnp.exp(z + tail + carries[h])
        if mask is not None:
            a = jnp.where(mask, a, 0.0)
        weights.append(a.astype(BF16))
        new_carries.append(carries[h] - jnp.sum(sp, axis=-1, keepdims=True))
    return tuple(new_carries), tuple(weights)


def _decode_chunk(qbd, bias, k_pages, v_pages, carry, acc, neg_from):
    n = len(k_pages)

    def lanes(x, i):
        return x[:, i * PAGE:(i + 1) * PAGE]

    k_all = jnp.concatenate([kp[...].astype(BF16) for kp in k_pages], axis=1)
    z = _dot(qbd, k_all) + bias
    sp = _softplus(z)
    sp_rows = jnp.concatenate([lanes(sp, i) for i in range(n)], axis=0)
    hi, lo = _split2(sp_rows)
    tail = _dot(hi, neg_from) + _dot(lo, neg_from)
    tot = jnp.sum(sp_rows, axis=-1, keepdims=True)
    weights = []
    for i in range(n):
        rows = slice(i * SB_HEADS, (i + 1) * SB_HEADS)
        weights.append(jnp.exp(lanes(z, i) + tail[rows] + carry))
        carry = carry - tot[rows]
    new_acc = []
    for h in range(SB_HEADS):
        rows = slice(h * SB_DH, (h + 1) * SB_DH)
        t = acc[h]
        for i in range(n):
            t = t + v_pages[i][rows, :] * jnp.broadcast_to(weights[i][h:h + 1, :], (SB_DH, PAGE))
        new_acc.append(t)
    return carry, new_acc


def _sb_kernel(bias_ref, pt_ref,
               q_ref, k_ref, v_ref, km_ref, vm_ref, gate_ref,
               qd_ref, gated_ref, biasd_ref, ck_ref, cv_ref,
               o_ref, od_ref,
               kb_ref, v0_ref, v1_ref, kmb_ref, vm0_ref, vm1_ref,
               kbuf, vbuf, sem, acc_ref, carry_ref):
    p = pl.program_id(1)
    qi = pl.program_id(2)
    step = (pl.program_id(0) * pl.num_programs(1) + p) * pl.num_programs(2) + qi
    n_steps = pl.num_programs(0) * pl.num_programs(1) * pl.num_programs(2)
    step_pages = kbuf.shape[1]
    n_pages = pt_ref.shape[1]
    steps_per_sample = n_pages // step_pages

    def page_copies(st, slot):
        sample = st // steps_per_sample
        first = (st % steps_per_sample) * step_pages
        copies = []
        for i in range(step_pages):
            page = pt_ref[sample, n_pages - 1 - (first + i)]
            copies.append(pltpu.make_async_copy(ck_ref.at[page], kbuf.at[slot, i], sem.at[0, slot]))
            copies.append(pltpu.make_async_copy(cv_ref.at[page], vbuf.at[slot, i], sem.at[1, slot]))
        return copies

    slot = step % 2

    @pl.when(step == 0)
    def _():
        for c in page_copies(step, slot):
            c.start()

    lane = lax.broadcasted_iota(jnp.int32, (1, 2 * SB_DH), 1)
    head0 = lane < SB_DH

    @pl.when(qi == 0)
    def _():
        kb_ref[...] = k_ref[0].astype(BF16)
        v = v_ref[0]
        v0_ref[...] = jnp.where(head0, v, 0.0).astype(BF16)
        v1_ref[...] = jnp.where(head0, 0.0, v).astype(BF16)
        kmb_ref[...] = km_ref[...].astype(BF16)
        vm = vm_ref[...]
        vm0_ref[...] = jnp.where(head0, vm, 0.0).astype(BF16)
        vm1_ref[...] = jnp.where(head0, 0.0, vm).astype(BF16)

    q = q_ref[0] * (SB_DH ** -0.5)
    q2 = jnp.concatenate([jnp.where(head0, q, 0.0), jnp.where(head0, 0.0, q)], axis=0).astype(BF16)
    biases = (bias_ref[2 * p], bias_ref[2 * p + 1])
    r = lax.broadcasted_iota(jnp.int32, (SB_BLK, SB_BLK), 0)
    s = lax.broadcasted_iota(jnp.int32, (SB_BLK, SB_BLK), 1)
    neg_from = jnp.where(r >= s, -1.0, 0.0).astype(BF16)
    diag_mask = s < r
    rm = lax.broadcasted_iota(jnp.int32, (PAGE, PAGE), 0)
    sm = lax.broadcasted_iota(jnp.int32, (PAGE, PAGE), 1)
    neg_from_m = jnp.where(rm >= sm, -1.0, 0.0).astype(BF16)
    meta_mask = lax.broadcasted_iota(jnp.int32, (SB_BLK, PAGE), 1) < N_META

    def scores(blk):
        return _dot_nt(q2, kb_ref[pl.ds(pl.multiple_of(blk * SB_BLK, SB_BLK), SB_BLK), :])

    def weighted_values(acc, weights, blk):
        rows = pl.ds(pl.multiple_of(blk * SB_BLK, SB_BLK), SB_BLK)
        return acc + _dot(weights[0], v0_ref[rows, :]) + _dot(weights[1], v1_ref[rows, :])

    ahead = jnp.where(step + 1 < n_steps, step + 1, 0)
    for c in page_copies(ahead, 1 - slot):
        c.start()

    zero_c = jnp.zeros((SB_BLK, 1), F32)
    z_meta = _dot_nt(q2, kmb_ref[...])
    carries, weights = _sb_weights(scores(qi), biases, (zero_c, zero_c), diag_mask, neg_from)
    z_next = scores(jnp.maximum(qi - 1, 0))

    def body(n, state):
        carries, acc, z_cur, w_prev = state
        acc = weighted_values(acc, w_prev, qi - n)
        z_nxt = scores(jnp.maximum(qi - 2 - n, 0))
        carries, w_cur = _sb_weights(z_cur, biases, carries, None, neg_from)
        return carries, acc, z_nxt, w_cur

    carries, acc, _, weights = lax.fori_loop(
        0, qi, body, (carries, jnp.zeros((SB_BLK, 2 * SB_DH), F32), z_next, weights))
    acc = weighted_values(acc, weights, 0)

    for c in page_copies(step, slot):
        c.wait()
    part = step % steps_per_sample

    @pl.when(part == 0)
    def _():
        acc_ref[...] = jnp.zeros_like(acc_ref)
        carry_ref[...] = jnp.zeros_like(carry_ref)

    width = SB_HEADS * SB_DH
    hrow = lax.broadcasted_iota(jnp.int32, (SB_HEADS, width), 0)
    hlane = lax.broadcasted_iota(jnp.int32, (SB_HEADS, width), 1) // SB_DH
    own = hrow == hlane
    qd = jnp.broadcast_to(qd_ref[0] * (SB_DH ** -0.5), (SB_HEADS, width))
    qbd = jnp.where(own, qd, 0.0).astype(BF16)
    carry_d = carry_ref[...]
    acc_d = [acc_ref[h * SB_DH:(h + 1) * SB_DH, :] for h in range(SB_HEADS)]
    for c0 in range(0, step_pages, DEC_PAGES):
        idx = range(c0, c0 + DEC_PAGES)
        carry_d, acc_d = _decode_chunk(qbd, biasd_ref[...], [kbuf.at[slot, i] for i in idx],
                                       [vbuf.at[slot, i] for i in idx], carry_d, acc_d, neg_from_m)
    for h in range(SB_HEADS):
        acc_ref[h * SB_DH:(h + 1) * SB_DH, :] = acc_d[h]
    carry_ref[...] = carry_d

    @pl.when(part == steps_per_sample - 1)
    def _():
        od = jnp.sum(acc_ref[...].T, axis=0, keepdims=True)
        od_ref[0] = od * gated_ref[0]

    _, w_meta = _sb_weights(z_meta, biases, carries, meta_mask, neg_from_m)
    acc = acc + _dot(w_meta[0], vm0_ref[...]) + _dot(w_meta[1], vm1_ref[...])
    o_ref[0] = acc * gate_ref[0]

    @pl.when(step == n_steps - 1)
    def _():
        for c in page_copies(ahead, 1 - slot):
            c.wait()


def _merge_kernel(x_ref, a_ref, b_ref, w_ref, y_ref):
    half = a_ref.shape[-1]
    y = _dot(a_ref[...].astype(BF16), w_ref[0:half, :])
    y = y + _dot(b_ref[...].astype(BF16), w_ref[half:2 * half, :])
    y_ref[...] = x_ref[...] + y


def _merge(x2d, a, b, w_bf16, tm):
    rows, d = x2d.shape
    half = a.shape[-1]
    return pl.pallas_call(
        _merge_kernel,
        grid=(rows // tm,),
        in_specs=[
            pl.BlockSpec((tm, d), lambda i: (i, 0)),
            pl.BlockSpec((tm, half), lambda i: (i, 0)),
            pl.BlockSpec((tm, half), lambda i: (i, 0)),
            pl.BlockSpec(w_bf16.shape, lambda i: (0, 0)),
        ],
        out_specs=pl.BlockSpec((tm, d), lambda i: (i, 0)),
        out_shape=jax.ShapeDtypeStruct((rows, d), F32),
        compiler_params=pltpu.CompilerParams(
            dimension_semantics=("parallel",), vmem_limit_bytes=VMEM_LIMIT),
        name="merge",
    )(x2d, a, b, w_bf16)


def kernel(x_prompt, x_sample, cache_k, cache_v, state_hgrn, page_table, meta_tokens, norm_g,
           w_in, lb_logits, hg_norm_g, q_norm_g, k_norm_g, sb_bias, w_out):
    nb, seq, d = x_prompt.shape
    db = x_sample.shape[0]
    n_pages = page_table.shape[1]
    n_phys = cache_k.shape[1]
    assert w_in.shape[0] == 1 and w_in.shape[2] == N_SEG * SEG
    assert seq % SB_BLK == 0 and seq % HG_CHUNK == 0 and nb % HG_BATCH == 0
    assert db % STEP_SAMPLES == 0 and (nb * seq) % PROJ_ROWS == 0

    w_in_b = w_in[0].astype(BF16)
    w_out_b = w_out[0].astype(BF16)
    g_row = norm_g[0][None, :]
    qg = jnp.tile(q_norm_g[0], SB_HEADS)[None, :]
    kg = jnp.tile(k_norm_g[0], SB_HEADS)[None, :]
    ng = hg_norm_g[0].reshape(1, HG_HEADS * HG_DIM)
    lane_head = jnp.arange(MXU_TILE) // SB_DH
    grp = (lane_head[:, None] == lane_head[None, :]).astype(BF16)
    bias = sb_bias[0].astype(F32)

    x2d = x_prompt.reshape(nb * seq, d)
    hq, lf, hv, hgate, sq, sk, sv, sgate = _project(x2d, PROJ_ROWS, g_row, w_in_b, lb_logits, qg, kg, grp)
    xs2d = x_sample.reshape(db, d)
    small = jnp.concatenate([xs2d, meta_tokens.astype(F32)], axis=0)
    n_small = small.shape[0]
    s_out = _project(small, n_small, g_row, w_in_b, lb_logits, qg, kg, grp)
    hq_s, lf_s, hv_s, hgate_s, sq_s, sk_s, sv_s, sgate_s = [a[:db] for a in s_out]
    _, lf_m, hv_m, _, _, sk_m, sv_m, _ = [a[db:] for a in s_out]

    pad_m = ((0, HG_CHUNK - N_META), (0, 0))
    st_meta = pl.pallas_call(
        _hgrn_meta_kernel,
        out_shape=jax.ShapeDtypeStruct((HG_HEADS, HG_DIM, HG_DIM), F32),
        name="hgrn_meta",
    )(jnp.pad(lf_m, pad_m), jnp.pad(hv_m, pad_m), _hgrn_sum_matrix(HG_CHUNK, False))

    def r3(a):
        return a.reshape(nb, seq, SEG)

    rspec = pl.BlockSpec((STEP_SAMPLES, SEG), lambda i: (i, 0))
    sspec = pl.BlockSpec((STEP_SAMPLES, HG_HEADS, HG_DIM, HG_DIM), lambda i: (i, 0, 0, 0))
    s_sample, a_hg_s = pl.pallas_call(
        _hgrn_step_kernel,
        grid=(db // STEP_SAMPLES,),
        in_specs=[rspec, rspec, rspec, rspec, pl.BlockSpec((1, SEG), lambda i: (0, 0)), sspec],
        out_specs=[sspec, rspec],
        out_shape=[jax.ShapeDtypeStruct((db, HG_HEADS, HG_DIM, HG_DIM), F32),
                   jax.ShapeDtypeStruct((db, SEG), F32)],
        compiler_params=pltpu.CompilerParams(
            dimension_semantics=("parallel",), vmem_limit_bytes=VMEM_LIMIT),
        name="hgrn_step",
    )(hq_s, lf_s, hv_s, hgate_s, ng, state_hgrn[0])

    pad_k = ((0, PAGE - N_META), (0, 0))
    n_qb = seq // SB_BLK
    n_pairs = SB_HEADS // 2
    n_steps = nb * n_pairs * n_qb
    assert (db * n_pages) % n_steps == 0
    step_pages = db * n_pages // n_steps
    assert n_pages % step_pages == 0 and step_pages % DEC_PAGES == 0
    steps_per_sample = n_pages // step_pages
    ck = jnp.transpose(cache_k[0], (0, 2, 3, 1)).reshape(n_phys, SEG, PAGE)
    cv = jnp.transpose(cache_v[0], (0, 2, 3, 1)).reshape(n_phys, SEG, PAGE)
    bias_bc = jnp.broadcast_to(bias[:, None], (SB_HEADS, DEC_PAGES * PAGE))

    qspec = pl.BlockSpec((1, SB_BLK, 2 * SB_DH), lambda b, p, i, *_: (b, i, p))
    kvspec = pl.BlockSpec((1, seq, 2 * SB_DH), lambda b, p, i, *_: (b, 0, p))
    mspec = pl.BlockSpec((PAGE, 2 * SB_DH), lambda b, p, i, *_: (0, p))
    vec_spec = pl.BlockSpec(
        (1, 1, SEG), lambda b, p, i, *_: (((b * n_pairs + p) * n_qb + i) // steps_per_sample, 0, 0))
    hbm_spec = pl.BlockSpec(memory_space=pl.ANY)
    b_sb, b_sb_s = pl.pallas_call(
        _sb_kernel,
        grid_spec=pltpu.PrefetchScalarGridSpec(
            num_scalar_prefetch=2,
            grid=(nb, n_pairs, n_qb),
            in_specs=[qspec, kvspec, kvspec, mspec, mspec, qspec,
                      vec_spec, vec_spec,
                      pl.BlockSpec((SB_HEADS, DEC_PAGES * PAGE), lambda b, p, i, *_: (0, 0)),
                      hbm_spec, hbm_spec],
            out_specs=[qspec, vec_spec],
            scratch_shapes=[pltpu.VMEM((seq, 2 * SB_DH), BF16)] * 3
                           + [pltpu.VMEM((PAGE, 2 * SB_DH), BF16)] * 3
                           + [pltpu.VMEM((2, step_pages, SEG, PAGE), F32)] * 2
                           + [pltpu.SemaphoreType.DMA((2, 2)),
                              pltpu.VMEM((SEG, PAGE), F32), pltpu.VMEM((SB_HEADS, PAGE), F32)],
        ),
        out_shape=[jax.ShapeDtypeStruct((nb, seq, SEG), F32),
                   jax.ShapeDtypeStruct((db, 1, SEG), F32)],
        compiler_params=pltpu.CompilerParams(
            dimension_semantics=("arbitrary", "arbitrary", "arbitrary"),
            vmem_limit_bytes=SB_VMEM_LIMIT),
        name="sb_attention",
    )(bias, page_table,
      r3(sq), r3(sk), r3(sv), jnp.pad(sk_m, pad_k), jnp.pad(sv_m, pad_k), r3(sgate),
      sq_s.reshape(db, 1, SEG), sgate_s.reshape(db, 1, SEG), bias_bc, ck, cv)

    n_chunks = seq // HG_CHUNK
    sums = _hgrn_sum_matrix(HG_CHUNK, True)
    lvl = _hgrn_level_index(HG_CHUNK)
    cspec = pl.BlockSpec((HG_BATCH, HG_CHUNK, SEG), lambda b, c: (b, c, 0))
    xspec = pl.BlockSpec((HG_BATCH, HG_CHUNK, d), lambda b, c: (b, c, 0))
    y_prompt, s_prompt = pl.pallas_call(
        _hgrn_prompt_kernel,
        grid=(nb // HG_BATCH, n_chunks),
        in_specs=[cspec, cspec, cspec, cspec,
                  pl.BlockSpec((1, SEG), lambda b, c: (0, 0)),
                  pl.BlockSpec((HG_HEADS, HG_DIM, HG_DIM), lambda b, c: (0, 0, 0)),
                  pl.BlockSpec(sums.shape, lambda b, c: (0, 0)),
                  pl.BlockSpec(lvl.shape, lambda b, c: (0, 0)),
                  xspec, cspec,
                  pl.BlockSpec(w_out_b.shape, lambda b, c: (0, 0))],
        out_specs=[xspec,
                   pl.BlockSpec((HG_BATCH, HG_HEADS, HG_DIM, HG_DIM), lambda b, c: (b, 0, 0, 0))],
        out_shape=[jax.ShapeDtypeStruct((nb, seq, d), F32),
                   jax.ShapeDtypeStruct((nb, HG_HEADS, HG_DIM, HG_DIM), F32)],
        scratch_shapes=[pltpu.VMEM((HG_BATCH, HG_HEADS, HG_DIM, HG_DIM), F32)],
        compiler_params=pltpu.CompilerParams(
            dimension_semantics=("parallel", "arbitrary"), vmem_limit_bytes=VMEM_LIMIT),
        name="hgrn_prompt",
    )(r3(hq), r3(lf), r3(hv), r3(hgate), ng, st_meta, sums, lvl, x_prompt, b_sb, w_out_b)

    y_sample = _merge(xs2d, a_hg_s, b_sb_s.reshape(db, SEG), w_out_b, db)

    def with_meta(meta_rows, real):
        m = jnp.broadcast_to(meta_rows[None], (nb, N_META, SEG))
        return jnp.concatenate([m, r3(real)], axis=1).reshape(1, nb, N_META + seq, SB_HEADS, SB_DH)

    return (y_prompt,
            y_sample.reshape(db, 1, d),
            with_meta(sk_m, sk),
            with_meta(sv_m, sv),
            s_prompt[None],
            sk_s.reshape(1, db, 1, SB_HEADS, SB_DH),
            sv_s.reshape(1, db, 1, SB_HEADS, SB_DH),
            s_sample[None])
```

```python
import jax
import jax.numpy as jnp
import numpy as np
from jax import lax
from jax.experimental import pallas as pl
from jax.experimental.pallas import tpu as pltpu

F32 = jnp.float32
BF16 = jnp.bfloat16
EPS = 1e-6

SEG = 512
N_SEG = 8
HG_HEADS = 4
HG_DIM = 128
SB_HEADS = 8
SB_DH = 64
PAGE = 128
MXU_TILE = 256
SUBLANES = 8
N_META = 16

HG_CHUNK = 128
HG_SMALL_LEVELS = (4, 2, 1)
HG_BATCH = 2
SB_BLK = 256
PROJ_ROWS = 512
DEC_PAGES = 16
STEP_SAMPLES = 8

VMEM_LIMIT = 56 * 1024 * 1024
SB_VMEM_LIMIT = 56 * 1024 * 1024

NT_DIMS = (((1,), (1,)), ((), ()))


def _dot(a, b):
    return jnp.dot(a, b, preferred_element_type=F32)


def _dot_nt(a, b):
    return lax.dot_general(a, b, NT_DIMS, preferred_element_type=F32)


def _split2(x):
    hi = x.astype(BF16)
    lo = (x - hi.astype(F32)).astype(BF16)
    return hi, lo


def _split3(x):
    hi = x.astype(BF16)
    r = x - hi.astype(F32)
    mid = r.astype(BF16)
    lo = (r - mid.astype(F32)).astype(BF16)
    return hi, mid, lo


def _softplus(z):
    return jnp.maximum(z, 0.0) + jnp.log(1.0 + jnp.exp(-jnp.abs(z)))


def _silu(z):
    return z * (1.0 / (1.0 + jnp.exp(-z)))


def _proj_kernel(x_ref, g_ref, w_ref, lbl_ref, qg_ref, kg_ref, grp_ref,
                 hq_ref, lf_ref, hv_ref, hgate_ref, sq_ref, sk_ref, sv_ref, sgate_ref):
    x = x_ref[...]
    ms = jnp.mean(x * x, axis=-1, keepdims=True)
    h = (x * lax.rsqrt(ms + EPS) * g_ref[...]).astype(BF16)

    def seg(j):
        return _dot(h, w_ref[:, j * SEG:(j + 1) * SEG])

    def group_rms(u, gain):
        hi, lo = _split2(u * u)
        gw = grp_ref.shape[0]
        ss = jnp.concatenate(
            [_dot(hi[:, c:c + gw], grp_ref[...]) + _dot(lo[:, c:c + gw], grp_ref[...])
             for c in range(0, SEG, gw)], axis=1)
        return u * lax.rsqrt(ss * (1.0 / SB_DH) + EPS) * gain

    hq_ref[...] = _silu(seg(0))

    l = lbl_ref[...]
    e = jnp.exp(l - jnp.max(l, axis=0, keepdims=True))
    lb = e[0:1, :] / jnp.sum(e, axis=0, keepdims=True)
    a = jnp.log(lb)
    z = seg(1)
    bb = jnp.log1p(-lb) + (jnp.minimum(z, 0.0) - jnp.log(1.0 + jnp.exp(-jnp.abs(z))))
    lf_ref[...] = jnp.maximum(a, bb) + jnp.log(1.0 + jnp.exp(-jnp.abs(a - bb)))

    hv_ref[...] = seg(2)
    hgate_ref[...] = _silu(seg(3))
    sq_ref[...] = group_rms(seg(4), qg_ref[...])
    sk_ref[...] = group_rms(seg(5), kg_ref[...]).reshape(sk_ref.shape)
    sv_ref[...] = seg(6).reshape(sv_ref.shape)
    sgate_ref[...] = _silu(seg(7))


def _project(x2d, tm, norm_g, w_bf16, lb_logits, qg, kg, grp, seq_rows=None, lead_rows=0):
    rows, d = x2d.shape
    const = lambda i: (0, 0)
    row_spec = pl.BlockSpec((tm, SEG), lambda i: (i, 0))
    row_shape = jax.ShapeDtypeStruct((rows, SEG), F32)
    kv_spec, kv_shape = row_spec, row_shape
    if seq_rows is not None:
        tiles = seq_rows // tm
        kv_spec = pl.BlockSpec((pl.Element(1), pl.Element(tm), pl.Element(SEG)),
                               lambda i: (i // tiles,
                                          pl.multiple_of(lead_rows + (i % tiles) * tm, SUBLANES), 0))
        kv_shape = jax.ShapeDtypeStruct((rows // seq_rows, lead_rows + seq_rows, SEG), F32)
    return pl.pallas_call(
        _proj_kernel,
        grid=(rows // tm,),
        in_specs=[
            pl.BlockSpec((tm, d), lambda i: (i, 0)),
            pl.BlockSpec((1, d), const),
            pl.BlockSpec(w_bf16.shape, const),
            pl.BlockSpec(lb_logits.shape, const),
            pl.BlockSpec((1, SEG), const),
            pl.BlockSpec((1, SEG), const),
            pl.BlockSpec(grp.shape, const),
        ],
        out_specs=[row_spec] * 5 + [kv_spec, kv_spec, row_spec],
        out_shape=[row_shape] * 5 + [kv_shape, kv_shape, row_shape],
        compiler_params=pltpu.CompilerParams(
            dimension_semantics=("parallel",), vmem_limit_bytes=VMEM_LIMIT),
        name="proj",
    )(x2d, norm_g, w_bf16, lb_logits, qg, kg, grp)


def _fill_lead_rows_kernel(mk_ref, mv_ref, k_in, v_in, k_ref, v_ref):
    del k_in, v_in
    k_ref[0] = mk_ref[...]
    v_ref[0] = mv_ref[...]


def _fill_lead_rows(mk, mv, k_full, v_full):
    lead = mk.shape[0]
    small = pl.BlockSpec(mk.shape, lambda b: (0, 0))
    hbm = pl.BlockSpec(memory_space=pl.ANY)
    top = pl.BlockSpec((1, lead, SEG), lambda b: (b, 0, 0))
    return pl.pallas_call(
        _fill_lead_rows_kernel,
        grid=(k_full.shape[0],),
        in_specs=[small, small, hbm, hbm],
        out_specs=[top, top],
        out_shape=[jax.ShapeDtypeStruct(k_full.shape, F32), jax.ShapeDtypeStruct(v_full.shape, F32)],
        input_output_aliases={2: 0, 3: 1},
        name="fill_meta_rows",
    )(mk, mv, k_full, v_full)


def _hgrn_sum_matrix(c, with_small_levels):
    t = np.arange(c)[:, None]
    j = np.arange(c)[None, :]
    blocks = [j <= t]
    if with_small_levels:
        for m in HG_SMALL_LEVELS[:-1]:
            f = (t // m) * m
            blocks.append((j > f) & (j <= t))
            blocks.append((j > t) & (j <= f + m))
        blocks.append(j == t + 1)
    return jnp.asarray(np.concatenate(blocks, axis=0), dtype=BF16)


def _hgrn_level_index(c):
    t = np.arange(c)[:, None]
    s = np.arange(c)[None, :]
    x = t ^ s
    lvl = np.where(x > 0, 2 ** np.floor(np.log2(np.maximum(x, 1))).astype(np.int64), 0)
    return jnp.asarray(np.where(s > t, -1, lvl), dtype=jnp.int32)


def _hgrn_chunk(q, lf, v, st, sums, lvl):
    c = q.shape[0]
    width = q.shape[1]
    want_out = lvl is not None
    lf_hi, lf_lo = _split2(lf)
    summed = _dot(sums, lf_hi) + _dot(sums, lf_lo)

    def block(i):
        return summed[i * c:(i + 1) * c, :]

    b = block(0)
    kk = 1.0 - jnp.exp(lf)
    b_last = b[c - 1:c, :]
    kdec = (kk * jnp.exp(jnp.minimum(b_last - b, 0.0))).astype(BF16)
    sdec = jnp.exp(b_last)

    new_st = []
    for h in range(HG_HEADS):
        sl = slice(h * HG_DIM, (h + 1) * HG_DIM)
        vt = v[:, sl].T.astype(BF16)
        new_st.append(st[h] * sdec[:, sl] + _dot(vt, kdec[:, sl]))
    if not want_out:
        return new_st, None

    levels = [(0, q, kk)]
    m = c // 2
    while m > HG_SMALL_LEVELS[0]:
        nb = c // m
        starts = [b[i * m:i * m + 1, :] for i in range(nb)]
        hold = jnp.concatenate([jnp.broadcast_to(starts[i], (m, width)) for i in range(nb)], axis=0)
        nxt = jnp.concatenate(
            [jnp.broadcast_to(starts[min(i + 1, nb - 1)], (m, width)) for i in range(nb)], axis=0)
        levels.append((m, q * jnp.exp(b - hold), kk * jnp.exp(jnp.minimum(nxt - b, 0.0))))
        m //= 2
    for i, m in enumerate(HG_SMALL_LEVELS[:-1]):
        levels.append((m, q * jnp.exp(block(1 + 2 * i)), kk * jnp.exp(block(2 + 2 * i))))
    levels.append((1, q, kk * jnp.exp(block(2 * len(HG_SMALL_LEVELS) - 1))))

    masks = [lvl == m for m, _, _ in levels]
    qdec = (q * jnp.exp(b)).astype(BF16)
    levels = [(qm.astype(BF16), km.astype(BF16)) for _, qm, km in levels]

    outs = []
    for h in range(HG_HEADS):
        sl = slice(h * HG_DIM, (h + 1) * HG_DIM)
        a = jnp.zeros((c, c), F32)
        for mask, (qm, km) in zip(masks, levels):
            a = jnp.where(mask, _dot_nt(qm[:, sl], km[:, sl]), a)
        o = _dot_nt(qdec[:, sl], st[h].astype(BF16)) + _dot(a.astype(BF16), v[:, sl].astype(BF16))
        outs.append(o)
    return new_st, outs


def _hgrn_meta_kernel(lf_ref, v_ref, sums_ref, st_ref):
    st0 = [jnp.zeros((HG_DIM, HG_DIM), F32)] * HG_HEADS
    lf = lf_ref[...]
    new_st, _ = _hgrn_chunk(lf, lf, v_ref[...], st0, sums_ref[...], None)
    for h in range(HG_HEADS):
        st_ref[h] = new_st[h]


def _hgrn_prompt_kernel(q_ref, lf_ref, v_ref, gate_ref, ng_ref, st0_ref, sums_ref, lvl_ref,
                        x_ref, sb_ref, w_ref, y_ref, s_ref, st_ref):
    ci = pl.program_id(1)
    n_batch = q_ref.shape[0]

    @pl.when(ci == 0)
    def _():
        for i in range(n_batch):
            st_ref[i] = st0_ref[...]

    final = []
    for i in range(n_batch):
        st = [st_ref[i, h] for h in range(HG_HEADS)]
        new_st, outs = _hgrn_chunk(q_ref[i], lf_ref[i], v_ref[i], st, sums_ref[...], lvl_ref[...])
        final.append(new_st)
        gated = []
        for h in range(HG_HEADS):
            sl = slice(h * HG_DIM, (h + 1) * HG_DIM)
            st_ref[i, h] = new_st[h]
            o = outs[h]
            ms = jnp.mean(o * o, axis=-1, keepdims=True)
            gated.append((o * lax.rsqrt(ms + EPS) * ng_ref[:, sl] * gate_ref[i, :, sl]).astype(BF16))
        a = jnp.concatenate(gated, axis=1)
        half = a.shape[1]
        y = _dot(a, w_ref[0:half, :]) + _dot(sb_ref[i].astype(BF16), w_ref[half:2 * half, :])
        y_ref[i] = x_ref[i] + y

    @pl.when(ci == pl.num_programs(1) - 1)
    def _():
        for i in range(n_batch):
            for h in range(HG_HEADS):
                s_ref[i, h] = final[i][h].T


def _hgrn_step_kernel(q_ref, lf_ref, v_ref, gate_ref, ng_ref, s0_ref, s_ref, a_ref):
    for i in range(STEP_SAMPLES):
        q = q_ref[i:i + 1, :]
        f = jnp.exp(lf_ref[i:i + 1, :])
        v = v_ref[i:i + 1, :]
        qk = q * (1.0 - f)
        for h in range(HG_HEADS):
            sl = slice(h * HG_DIM, (h + 1) * HG_DIM)
            fc = jnp.broadcast_to(f[:, sl], (HG_DIM, HG_DIM)).T
            sn = fc * s0_ref[i, h] + (1.0 - fc) * v[:, sl]
            s_ref[i, h] = sn
            qf = jnp.broadcast_to(q[:, sl] * f[:, sl], (8, HG_DIM)).astype(BF16)
            o = _dot(qf, s0_ref[i, h].astype(BF16))[0:1, :]
            o = o + jnp.sum(qk[:, sl], axis=-1, keepdims=True) * v[:, sl]
            ms = jnp.mean(o * o, axis=-1, keepdims=True)
            a_ref[i:i + 1, sl] = o * lax.rsqrt(ms + EPS) * ng_ref[:, sl] * gate_ref[i:i + 1, sl]


def _sb_weights(z2, biases, carries, mask, neg_from):
    tq = z2.shape[0] // 2
    new_carries, weights = [], []
    for h in range(2):
        z = z2[h * tq:(h + 1) * tq] + biases[h]
        sp = _softplus(z)
        if mask is not None:
            sp = jnp.where(mask, sp, 0.0)
        tail = _dot(sp.astype(BF16), neg_from)
        a = jnp.exp(z + tail + carries[h])
        if mask is not None:
            a = jnp.where(mask, a, 0.0)
        weights.append(a.astype(BF16))
        new_carries.append(carries[h] - jnp.sum(sp, axis=-1, keepdims=True))
    return tuple(new_carries), tuple(weights)


def _decode_chunk(qbd, bias, k_pages, v_pages, carry, acc, neg_from):
    n = len(k_pages)

    def lanes(x, i):
        return x[:, i * PAGE:(i + 1) * PAGE]

    k_all = jnp.concatenate([kp[...].astype(BF16) for kp in k_pages], axis=1)
    z = _dot(qbd, k_all) + bias
    sp = _softplus(z)
    sp_rows = jnp.concatenate([lanes(sp, i) for i in range(n)], axis=0)
    hi, lo = _split2(sp_rows)
    tail = _dot(hi, neg_from) + _dot(lo, neg_from)
    tot = jnp.sum(sp_rows, axis=-1, keepdims=True)
    weights = []
    for i in range(n):
        rows = slice(i * SB_HEADS, (i + 1) * SB_HEADS)
        weights.append(jnp.exp(lanes(z, i) + tail[rows] + carry))
        carry = carry - tot[rows]
    new_acc = []
    for h in range(SB_HEADS):
        rows = slice(h * SB_DH, (h + 1) * SB_DH)
        t = acc[h]
        for i in range(n):
            t = t + v_pages[i][rows, :] * jnp.broadcast_to(weights[i][h:h + 1, :], (SB_DH, PAGE))
        new_acc.append(t)
    return carry, new_acc


def _sb_kernel(bias_ref, pt_ref,
               q_ref, k_ref, v_ref, gate_ref,
               qd_ref, gated_ref, biasd_ref, ck_ref, cv_ref,
               o_ref, od_ref,
               kb_ref, v0_ref, v1_ref, kmb_ref, vm0_ref, vm1_ref,
               kbuf, vbuf, sem, acc_ref, carry_ref):
    p = pl.program_id(1)
    qi = pl.program_id(2)
    step = (pl.program_id(0) * pl.num_programs(1) + p) * pl.num_programs(2) + qi
    n_steps = pl.num_programs(0) * pl.num_programs(1) * pl.num_programs(2)
    step_pages = kbuf.shape[1]
    n_pages = pt_ref.shape[1]
    steps_per_sample = n_pages // step_pages

    def page_copies(st, slot):
        sample = st // steps_per_sample
        first = (st % steps_per_sample) * step_pages
        copies = []
        for i in range(step_pages):
            page = pt_ref[sample, n_pages - 1 - (first + i)]
            copies.append(pltpu.make_async_copy(ck_ref.at[page], kbuf.at[slot, i], sem.at[0, slot]))
            copies.append(pltpu.make_async_copy(cv_ref.at[page], vbuf.at[slot, i], sem.at[1, slot]))
        return copies

    slot = step % 2

    @pl.when(step == 0)
    def _():
        for c in page_copies(step, slot):
            c.start()

    lane = lax.broadcasted_iota(jnp.int32, (1, 2 * SB_DH), 1)
    head0 = lane < SB_DH

    @pl.when(qi == 0)
    def _():
        kb_ref[...] = k_ref[0, N_META:, :].astype(BF16)
        v = v_ref[0, N_META:, :]
        v0_ref[...] = jnp.where(head0, v, 0.0).astype(BF16)
        v1_ref[...] = jnp.where(head0, 0.0, v).astype(BF16)
        for ref in (kmb_ref, vm0_ref, vm1_ref):
            ref[...] = jnp.zeros_like(ref)
        kmb_ref[0:N_META, :] = k_ref[0, 0:N_META, :].astype(BF16)
        vm = v_ref[0, 0:N_META, :]
        vm0_ref[0:N_META, :] = jnp.where(head0, vm, 0.0).astype(BF16)
        vm1_ref[0:N_META, :] = jnp.where(head0, 0.0, vm).astype(BF16)

    q = q_ref[0] * (SB_DH ** -0.5)
    q2 = jnp.concatenate([jnp.where(head0, q, 0.0), jnp.where(head0, 0.0, q)], axis=0).astype(BF16)
    biases = (bias_ref[2 * p], bias_ref[2 * p + 1])
    r = lax.broadcasted_iota(jnp.int32, (SB_BLK, SB_BLK), 0)
    s = lax.broadcasted_iota(jnp.int32, (SB_BLK, SB_BLK), 1)
    neg_from = jnp.where(r >= s, -1.0, 0.0).astype(BF16)
    diag_mask = s < r
    rm = lax.broadcasted_iota(jnp.int32, (PAGE, PAGE), 0)
    sm = lax.broadcasted_iota(jnp.int32, (PAGE, PAGE), 1)
    neg_from_m = jnp.where(rm >= sm, -1.0, 0.0).astype(BF16)
    meta_mask = lax.broadcasted_iota(jnp.int32, (SB_BLK, PAGE), 1) < N_META

    def scores(blk):
        return _dot_nt(q2, kb_ref[pl.ds(pl.multiple_of(blk * SB_BLK, SB_BLK), SB_BLK), :])

    def weighted_values(acc, weights, blk):
        rows = pl.ds(pl.multiple_of(blk * SB_BLK, SB_BLK), SB_BLK)
        return acc + _dot(weights[0], v0_ref[rows, :]) + _dot(weights[1], v1_ref[rows, :])

    ahead = jnp.where(step + 1 < n_steps, step + 1, 0)
    for c in page_copies(ahead, 1 - slot):
        c.start()

    zero_c = jnp.zeros((SB_BLK, 1), F32)
    z_meta = _dot_nt(q2, kmb_ref[...])
    carries, weights = _sb_weights(scores(qi), biases, (zero_c, zero_c), diag_mask, neg_from)
    z_next = scores(jnp.maximum(qi - 1, 0))

    def body(n, state):
        carries, acc, z_cur, w_prev = state
        acc = weighted_values(acc, w_prev, qi - n)
        z_nxt = scores(jnp.maximum(qi - 2 - n, 0))
        carries, w_cur = _sb_weights(z_cur, biases, carries, None, neg_from)
        return carries, acc, z_nxt, w_cur

    carries, acc, _, weights = lax.fori_loop(
        0, qi, body, (carries, jnp.zeros((SB_BLK, 2 * SB_DH), F32), z_next, weights))
    acc = weighted_values(acc, weights, 0)

    for c in page_copies(step, slot):
        c.wait()
    part = step % steps_per_sample

    @pl.when(part == 0)
    def _():
        acc_ref[...] = jnp.zeros_like(acc_ref)
        carry_ref[...] = jnp.zeros_like(carry_ref)

    width = SB_HEADS * SB_DH
    hrow = lax.broadcasted_iota(jnp.int32, (SB_HEADS, width), 0)
    hlane = lax.broadcasted_iota(jnp.int32, (SB_HEADS, width), 1) // SB_DH
    own = hrow == hlane
    qd = jnp.broadcast_to(qd_ref[0] * (SB_DH ** -0.5), (SB_HEADS, width))
    qbd = jnp.where(own, qd, 0.0).astype(BF16)
    carry_d = carry_ref[...]
    acc_d = [acc_ref[h * SB_DH:(h + 1) * SB_DH, :] for h in range(SB_HEADS)]
    for c0 in range(0, step_pages, DEC_PAGES):
        idx = range(c0, c0 + DEC_PAGES)
        carry_d, acc_d = _decode_chunk(qbd, biasd_ref[...], [kbuf.at[slot, i] for i in idx],
                                       [vbuf.at[slot, i] for i in idx], carry_d, acc_d, neg_from_m)
    for h in range(SB_HEADS):
        acc_ref[h * SB_DH:(h + 1) * SB_DH, :] = acc_d[h]
    carry_ref[...] = carry_d

    @pl.when(part == steps_per_sample - 1)
    def _():
        od = jnp.sum(acc_ref[...].T, axis=0, keepdims=True)
        od_ref[0] = od * gated_ref[0]

    _, w_meta = _sb_weights(z_meta, biases, carries, meta_mask, neg_from_m)
    acc = acc + _dot(w_meta[0], vm0_ref[...]) + _dot(w_meta[1], vm1_ref[...])
    o_ref[0] = acc * gate_ref[0]

    @pl.when(step == n_steps - 1)
    def _():
        for c in page_copies(ahead, 1 - slot):
            c.wait()


def _merge_kernel(x_ref, a_ref, b_ref, w_ref, y_ref):
    half = a_ref.shape[-1]
    y = _dot(a_ref[...].astype(BF16), w_ref[0:half, :])
    y = y + _dot(b_ref[...].astype(BF16), w_ref[half:2 * half, :])
    y_ref[...] = x_ref[...] + y


def _merge(x2d, a, b, w_bf16, tm):
    rows, d = x2d.shape
    half = a.shape[-1]
    return pl.pallas_call(
        _merge_kernel,
        grid=(rows // tm,),
        in_specs=[
            pl.BlockSpec((tm, d), lambda i: (i, 0)),
            pl.BlockSpec((tm, half), lambda i: (i, 0)),
            pl.BlockSpec((tm, half), lambda i: (i, 0)),
            pl.BlockSpec(w_bf16.shape, lambda i: (0, 0)),
        ],
        out_specs=pl.BlockSpec((tm, d), lambda i: (i, 0)),
        out_shape=jax.ShapeDtypeStruct((rows, d), F32),
        compiler_params=pltpu.CompilerParams(
            dimension_semantics=("parallel",), vmem_limit_bytes=VMEM_LIMIT),
        name="merge",
    )(x2d, a, b, w_bf16)


def kernel(x_prompt, x_sample, cache_k, cache_v, state_hgrn, page_table, meta_tokens, norm_g,
           w_in, lb_logits, hg_norm_g, q_norm_g, k_norm_g, sb_bias, w_out):
    nb, seq, d = x_prompt.shape
    db = x_sample.shape[0]
    n_pages = page_table.shape[1]
    n_phys = cache_k.shape[1]
    assert w_in.shape[0] == 1 and w_in.shape[2] == N_SEG * SEG
    assert seq % SB_BLK == 0 and seq % HG_CHUNK == 0 and nb % HG_BATCH == 0
    assert db % STEP_SAMPLES == 0 and (nb * seq) % PROJ_ROWS == 0

    w_in_b = w_in[0].astype(BF16)
    w_out_b = w_out[0].astype(BF16)
    g_row = norm_g[0][None, :]
    qg = jnp.tile(q_norm_g[0], SB_HEADS)[None, :]
    kg = jnp.tile(k_norm_g[0], SB_HEADS)[None, :]
    ng = hg_norm_g[0].reshape(1, HG_HEADS * HG_DIM)
    lane_head = jnp.arange(MXU_TILE) // SB_DH
    grp = (lane_head[:, None] == lane_head[None, :]).astype(BF16)
    bias = sb_bias[0].astype(F32)

    x2d = x_prompt.reshape(nb * seq, d)
    hq, lf, hv, hgate, sq, sk_all, sv_all, sgate = _project(
        x2d, PROJ_ROWS, g_row, w_in_b, lb_logits, qg, kg, grp, seq_rows=seq, lead_rows=N_META)
    xs2d = x_sample.reshape(db, d)
    small = jnp.concatenate([xs2d, meta_tokens.astype(F32)], axis=0)
    n_small = small.shape[0]
    s_out = _project(small, n_small, g_row, w_in_b, lb_logits, qg, kg, grp)
    hq_s, lf_s, hv_s, hgate_s, sq_s, sk_s, sv_s, sgate_s = [a[:db] for a in s_out]
    _, lf_m, hv_m, _, _, sk_m, sv_m, _ = [a[db:] for a in s_out]
    sk_all, sv_all = _fill_lead_rows(sk_m, sv_m, sk_all, sv_all)

    pad_m = ((0, HG_CHUNK - N_META), (0, 0))
    st_meta = pl.pallas_call(
        _hgrn_meta_kernel,
        out_shape=jax.ShapeDtypeStruct((HG_HEADS, HG_DIM, HG_DIM), F32),
        name="hgrn_meta",
    )(jnp.pad(lf_m, pad_m), jnp.pad(hv_m, pad_m), _hgrn_sum_matrix(HG_CHUNK, False))

    def r3(a):
        return a.reshape(nb, seq, SEG)

    rspec = pl.BlockSpec((STEP_SAMPLES, SEG), lambda i: (i, 0))
    sspec = pl.BlockSpec((STEP_SAMPLES, HG_HEADS, HG_DIM, HG_DIM), lambda i: (i, 0, 0, 0))
    s_sample, a_hg_s = pl.pallas_call(
        _hgrn_step_kernel,
        grid=(db // STEP_SAMPLES,),
        in_specs=[rspec, rspec, rspec, rspec, pl.BlockSpec((1, SEG), lambda i: (0, 0)), sspec],
        out_specs=[sspec, rspec],
        out_shape=[jax.ShapeDtypeStruct((db, HG_HEADS, HG_DIM, HG_DIM), F32),
                   jax.ShapeDtypeStruct((db, SEG), F32)],
        compiler_params=pltpu.CompilerParams(
            dimension_semantics=("parallel",), vmem_limit_bytes=VMEM_LIMIT),
        name="hgrn_step",
    )(hq_s, lf_s, hv_s, hgate_s, ng, state_hgrn[0])

    n_qb = seq // SB_BLK
    n_pairs = SB_HEADS // 2
    n_steps = nb * n_pairs * n_qb
    assert (db * n_pages) % n_steps == 0
    step_pages = db * n_pages // n_steps
    assert n_pages % step_pages == 0 and step_pages % DEC_PAGES == 0
    steps_per_sample = n_pages // step_pages
    ck = jnp.transpose(cache_k[0], (0, 2, 3, 1)).reshape(n_phys, SEG, PAGE)
    cv = jnp.transpose(cache_v[0], (0, 2, 3, 1)).reshape(n_phys, SEG, PAGE)
    bias_bc = jnp.broadcast_to(bias[:, None], (SB_HEADS, DEC_PAGES * PAGE))

    qspec = pl.BlockSpec((1, SB_BLK, 2 * SB_DH), lambda b, p, i, *_: (b, i, p))
    kvspec = pl.BlockSpec((1, N_META + seq, 2 * SB_DH), lambda b, p, i, *_: (b, 0, p))
    vec_spec = pl.BlockSpec(
        (1, 1, SEG), lambda b, p, i, *_: (((b * n_pairs + p) * n_qb + i) // steps_per_sample, 0, 0))
    hbm_spec = pl.BlockSpec(memory_space=pl.ANY)
    b_sb, b_sb_s = pl.pallas_call(
        _sb_kernel,
        grid_spec=pltpu.PrefetchScalarGridSpec(
            num_scalar_prefetch=2,
            grid=(nb, n_pairs, n_qb),
            in_specs=[qspec, kvspec, kvspec, qspec,
                      vec_spec, vec_spec,
                      pl.BlockSpec((SB_HEADS, DEC_PAGES * PAGE), lambda b, p, i, *_: (0, 0)),
                      hbm_spec, hbm_spec],
            out_specs=[qspec, vec_spec],
            scratch_shapes=[pltpu.VMEM((seq, 2 * SB_DH), BF16)] * 3
                           + [pltpu.VMEM((PAGE, 2 * SB_DH), BF16)] * 3
                           + [pltpu.VMEM((2, step_pages, SEG, PAGE), F32)] * 2
                           + [pltpu.SemaphoreType.DMA((2, 2)),
                              pltpu.VMEM((SEG, PAGE), F32), pltpu.VMEM((SB_HEADS, PAGE), F32)],
        ),
        out_shape=[jax.ShapeDtypeStruct((nb, seq, SEG), F32),
                   jax.ShapeDtypeStruct((db, 1, SEG), F32)],
        compiler_params=pltpu.CompilerParams(
            dimension_semantics=("arbitrary", "arbitrary", "arbitrary"),
            vmem_limit_bytes=SB_VMEM_LIMIT),
        name="sb_attention",
    )(bias, page_table,
      r3(sq), sk_all, sv_all, r3(sgate),
      sq_s.reshape(db, 1, SEG), sgate_s.reshape(db, 1, SEG), bias_bc, ck, cv)

    n_chunks = seq // HG_CHUNK
    sums = _hgrn_sum_matrix(HG_CHUNK, True)
    lvl = _hgrn_level_index(HG_CHUNK)
    cspec = pl.BlockSpec((HG_BATCH, HG_CHUNK, SEG), lambda b, c: (b, c, 0))
    xspec = pl.BlockSpec((HG_BATCH, HG_CHUNK, d), lambda b, c: (b, c, 0))
    y_prompt, s_prompt = pl.pallas_call(
        _hgrn_prompt_kernel,
        grid=(nb // HG_BATCH, n_chunks),
        in_specs=[cspec, cspec, cspec, cspec,
                  pl.BlockSpec((1, SEG), lambda b, c: (0, 0)),
                  pl.BlockSpec((HG_HEADS, HG_DIM, HG_DIM), lambda b, c: (0, 0, 0)),
                  pl.BlockSpec(sums.shape, lambda b, c: (0, 0)),
                  pl.BlockSpec(lvl.shape, lambda b, c: (0, 0)),
                  xspec, cspec,
                  pl.BlockSpec(w_out_b.shape, lambda b, c: (0, 0))],
        out_specs=[xspec,
                   pl.BlockSpec((HG_BATCH, HG_HEADS, HG_DIM, HG_DIM), lambda b, c: (b, 0, 0, 0))],
        out_shape=[jax.ShapeDtypeStruct((nb, seq, d), F32),
                   jax.ShapeDtypeStruct((nb, HG_HEADS, HG_DIM, HG_DIM), F32)],
        scratch_shapes=[pltpu.VMEM((HG_BATCH, HG_HEADS, HG_DIM, HG_DIM), F32)],
        compiler_params=pltpu.CompilerParams(
            dimension_semantics=("parallel", "arbitrary"), vmem_limit_bytes=VMEM_LIMIT),
        name="hgrn_prompt",
    )(r3(hq), r3(lf), r3(hv), r3(hgate), ng, st_meta, sums, lvl, x_prompt, b_sb, w_out_b)

    y_sample = _merge(xs2d, a_hg_s, b_sb_s.reshape(db, SEG), w_out_b, db)

    return (y_prompt,
            y_sample.reshape(db, 1, d),
            sk_all.reshape(1, nb, N_META + seq, SB_HEADS, SB_DH),
            sv_all.reshape(1, nb, N_META + seq, SB_HEADS, SB_DH),
            s_prompt[None],
            sk_s.reshape(1, db, 1, SB_HEADS, SB_DH),
            sv_s.reshape(1, db, 1, SB_HEADS, SB_DH),
            s_sample[None])
```

```python
import functools

import jax
import jax.numpy as jnp
import numpy as np
from jax import lax
from jax.experimental import pallas as pl
from jax.experimental.pallas import tpu as pltpu

F32 = jnp.float32
BF16 = jnp.bfloat16
EPS = 1e-6

SEG = 512
N_SEG = 8
HG_HEADS = 4
HG_DIM = 128
SB_HEADS = 8
SB_DH = 64
PAGE = 128
MXU_TILE = 256
SUBLANES = 8
N_META = 16

HG_CHUNK = 128
HG_SMALL_LEVELS = (4, 2, 1)
HG_BATCH = 2
SB_BLK = 256
PROJ_ROWS = 512
DEC_PAGES = 16
STEP_SAMPLES = 8

VMEM_LIMIT = 56 * 1024 * 1024
SB_VMEM_LIMIT = 56 * 1024 * 1024

NT_DIMS = (((1,), (1,)), ((), ()))


def _dot(a, b):
    return jnp.dot(a, b, preferred_element_type=F32)


def _dot_nt(a, b):
    return lax.dot_general(a, b, NT_DIMS, preferred_element_type=F32)


def _split2(x):
    hi = x.astype(BF16)
    lo = (x - hi.astype(F32)).astype(BF16)
    return hi, lo


def _split3(x):
    hi = x.astype(BF16)
    r = x - hi.astype(F32)
    mid = r.astype(BF16)
    lo = (r - mid.astype(F32)).astype(BF16)
    return hi, mid, lo


def _softplus(z):
    return jnp.maximum(z, 0.0) + jnp.log(1.0 + jnp.exp(-jnp.abs(z)))


def _silu(z):
    return z * (1.0 / (1.0 + jnp.exp(-z)))


def _proj_kernel(x_ref, g_ref, w_ref, lbl_ref, qg_ref, kg_ref, grp_ref,
                 *rest, lead):
    lead_refs = rest[:-N_SEG]
    hq_ref, lf_ref, hv_ref, hgate_ref, sq_ref, sk_ref, sv_ref, sgate_ref = rest[-N_SEG:]
    ti = pl.program_id(1)
    tm = x_ref.shape[1]
    x = x_ref[0]
    ms = jnp.mean(x * x, axis=-1, keepdims=True)
    h = (x * lax.rsqrt(ms + EPS) * g_ref[...]).astype(BF16)

    def seg(j):
        return _dot(h, w_ref[:, j * SEG:(j + 1) * SEG])

    def group_rms(u, gain):
        hi, lo = _split2(u * u)
        gw = grp_ref.shape[0]
        ss = jnp.concatenate(
            [_dot(hi[:, c:c + gw], grp_ref[...]) + _dot(lo[:, c:c + gw], grp_ref[...])
             for c in range(0, SEG, gw)], axis=1)
        return u * lax.rsqrt(ss * (1.0 / SB_DH) + EPS) * gain

    hq_ref[0] = _silu(seg(0))

    l = lbl_ref[...]
    e = jnp.exp(l - jnp.max(l, axis=0, keepdims=True))
    lb = e[0:1, :] / jnp.sum(e, axis=0, keepdims=True)
    a = jnp.log(lb)
    z = seg(1)
    bb = jnp.log1p(-lb) + (jnp.minimum(z, 0.0) - jnp.log(1.0 + jnp.exp(-jnp.abs(z))))
    lf_ref[0] = jnp.maximum(a, bb) + jnp.log(1.0 + jnp.exp(-jnp.abs(a - bb)))

    hv_ref[0] = seg(2)
    hgate_ref[0] = _silu(seg(3))
    sq_ref[0] = group_rms(seg(4), qg_ref[...])
    sgate_ref[0] = _silu(seg(7))

    rows = pl.ds(pl.multiple_of(lead + ti * tm, SUBLANES), tm)
    sk_ref[0, rows, :] = group_rms(seg(5), kg_ref[...])
    sv_ref[0, rows, :] = seg(6)
    if lead:
        @pl.when(ti == 0)
        def _():
            sk_ref[0, 0:lead, :] = lead_refs[0][...]
            sv_ref[0, 0:lead, :] = lead_refs[1][...]


def _project(x3d, tm, norm_g, w_bf16, lb_logits, qg, kg, grp, lead_k=None, lead_v=None):
    n_seq, rows, d = x3d.shape
    lead = 0 if lead_k is None else lead_k.shape[0]
    lead_args = [] if lead_k is None else [lead_k, lead_v]
    const = lambda s, t: (0, 0)
    row_spec = pl.BlockSpec((1, tm, SEG), lambda s, t: (s, t, 0))
    row_shape = jax.ShapeDtypeStruct((n_seq, rows, SEG), F32)
    kv_spec = pl.BlockSpec((1, lead + rows, SEG), lambda s, t: (s, 0, 0))
    kv_shape = jax.ShapeDtypeStruct((n_seq, lead + rows, SEG), F32)
    return pl.pallas_call(
        functools.partial(_proj_kernel, lead=lead),
        grid=(n_seq, rows // tm),
        in_specs=[
            pl.BlockSpec((1, tm, d), lambda s, t: (s, t, 0)),
            pl.BlockSpec((1, d), const),
            pl.BlockSpec(w_bf16.shape, const),
            pl.BlockSpec(lb_logits.shape, const),
            pl.BlockSpec((1, SEG), const),
            pl.BlockSpec((1, SEG), const),
            pl.BlockSpec(grp.shape, const),
        ] + [pl.BlockSpec((lead, SEG), const)] * len(lead_args),
        out_specs=[row_spec] * 5 + [kv_spec, kv_spec, row_spec],
        out_shape=[row_shape] * 5 + [kv_shape, kv_shape, row_shape],
        compiler_params=pltpu.CompilerParams(
            dimension_semantics=("parallel", "arbitrary"), vmem_limit_bytes=VMEM_LIMIT),
        name="proj",
    )(x3d, norm_g, w_bf16, lb_logits, qg, kg, grp, *lead_args)


def _hgrn_sum_matrix(c, with_small_levels):
    t = np.arange(c)[:, None]
    j = np.arange(c)[None, :]
    blocks = [j <= t]
    if with_small_levels:
        for m in HG_SMALL_LEVELS[:-1]:
            f = (t // m) * m
            blocks.append((j > f) & (j <= t))
            blocks.append((j > t) & (j <= f + m))
        blocks.append(j == t + 1)
    return jnp.asarray(np.concatenate(blocks, axis=0), dtype=BF16)


def _hgrn_level_index(c):
    t = np.arange(c)[:, None]
    s = np.arange(c)[None, :]
    x = t ^ s
    lvl = np.where(x > 0, 2 ** np.floor(np.log2(np.maximum(x, 1))).astype(np.int64), 0)
    return jnp.asarray(np.where(s > t, -1, lvl), dtype=jnp.int32)


def _hgrn_chunk(q, lf, v, st, sums, lvl):
    c = q.shape[0]
    width = q.shape[1]
    want_out = lvl is not None
    lf_hi, lf_lo = _split2(lf)
    summed = _dot(sums, lf_hi) + _dot(sums, lf_lo)

    def block(i):
        return summed[i * c:(i + 1) * c, :]

    b = block(0)
    kk = 1.0 - jnp.exp(lf)
    b_last = b[c - 1:c, :]
    kdec = (kk * jnp.exp(jnp.minimum(b_last - b, 0.0))).astype(BF16)
    sdec = jnp.exp(b_last)

    new_st = []
    for h in range(HG_HEADS):
        sl = slice(h * HG_DIM, (h + 1) * HG_DIM)
        vt = v[:, sl].T.astype(BF16)
        new_st.append(st[h] * sdec[:, sl] + _dot(vt, kdec[:, sl]))
    if not want_out:
        return new_st, None

    levels = [(0, q, kk)]
    m = c // 2
    while m > HG_SMALL_LEVELS[0]:
        nb = c // m
        starts = [b[i * m:i * m + 1, :] for i in range(nb)]
        hold = jnp.concatenate([jnp.broadcast_to(starts[i], (m, width)) for i in range(nb)], axis=0)
        nxt = jnp.concatenate(
            [jnp.broadcast_to(starts[min(i + 1, nb - 1)], (m, width)) for i in range(nb)], axis=0)
        levels.append((m, q * jnp.exp(b - hold), kk * jnp.exp(jnp.minimum(nxt - b, 0.0))))
        m //= 2
    for i, m in enumerate(HG_SMALL_LEVELS[:-1]):
        levels.append((m, q * jnp.exp(block(1 + 2 * i)), kk * jnp.exp(block(2 + 2 * i))))
    levels.append((1, q, kk * jnp.exp(block(2 * len(HG_SMALL_LEVELS) - 1))))

    masks = [lvl == m for m, _, _ in levels]
    qdec = (q * jnp.exp(b)).astype(BF16)
    levels = [(qm.astype(BF16), km.astype(BF16)) for _, qm, km in levels]

    outs = []
    for h in range(HG_HEADS):
        sl = slice(h * HG_DIM, (h + 1) * HG_DIM)
        a = jnp.zeros((c, c), F32)
        for mask, (qm, km) in zip(masks, levels):
            a = jnp.where(mask, _dot_nt(qm[:, sl], km[:, sl]), a)
        o = _dot_nt(qdec[:, sl], st[h].astype(BF16)) + _dot(a.astype(BF16), v[:, sl].astype(BF16))
        outs.append(o)
    return new_st, outs


def _hgrn_meta_kernel(lf_ref, v_ref, sums_ref, st_ref):
    st0 = [jnp.zeros((HG_DIM, HG_DIM), F32)] * HG_HEADS
    lf = lf_ref[...]
    new_st, _ = _hgrn_chunk(lf, lf, v_ref[...], st0, sums_ref[...], None)
    for h in range(HG_HEADS):
        st_ref[h] = new_st[h]


def _hgrn_prompt_kernel(q_ref, lf_ref, v_ref, gate_ref, ng_ref, st0_ref, sums_ref, lvl_ref,
                        x_ref, sb_ref, w_ref, y_ref, s_ref, st_ref):
    ci = pl.program_id(1)
    n_batch = q_ref.shape[0]

    @pl.when(ci == 0)
    def _():
        for i in range(n_batch):
            st_ref[i] = st0_ref[...]

    final = []
    for i in range(n_batch):
        st = [st_ref[i, h] for h in range(HG_HEADS)]
        new_st, outs = _hgrn_chunk(q_ref[i], lf_ref[i], v_ref[i], st, sums_ref[...], lvl_ref[...])
        final.append(new_st)
        gated = []
        for h in range(HG_HEADS):
            sl = slice(h * HG_DIM, (h + 1) * HG_DIM)
            st_ref[i, h] = new_st[h]
            o = outs[h]
            ms = jnp.mean(o * o, axis=-1, keepdims=True)
            gated.append((o * lax.rsqrt(ms + EPS) * ng_ref[:, sl] * gate_ref[i, :, sl]).astype(BF16))
        a = jnp.concatenate(gated, axis=1)
        half = a.shape[1]
        y = _dot(a, w_ref[0:half, :]) + _dot(sb_ref[i].astype(BF16), w_ref[half:2 * half, :])
        y_ref[i] = x_ref[i] + y

    @pl.when(ci == pl.num_programs(1) - 1)
    def _():
        for i in range(n_batch):
            for h in range(HG_HEADS):
                s_ref[i, h] = final[i][h].T


def _hgrn_step_kernel(q_ref, lf_ref, v_ref, gate_ref, ng_ref, s0_ref, s_ref, a_ref):
    for i in range(STEP_SAMPLES):
        q = q_ref[i:i + 1, :]
        f = jnp.exp(lf_ref[i:i + 1, :])
        v = v_ref[i:i + 1, :]
        qk = q * (1.0 - f)
        for h in range(HG_HEADS):
            sl = slice(h * HG_DIM, (h + 1) * HG_DIM)
            fc = jnp.broadcast_to(f[:, sl], (HG_DIM, HG_DIM)).T
            sn = fc * s0_ref[i, h] + (1.0 - fc) * v[:, sl]
            s_ref[i, h] = sn
            qf = jnp.broadcast_to(q[:, sl] * f[:, sl], (8, HG_DIM)).astype(BF16)
            o = _dot(qf, s0_ref[i, h].astype(BF16))[0:1, :]
            o = o + jnp.sum(qk[:, sl], axis=-1, keepdims=True) * v[:, sl]
            ms = jnp.mean(o * o, axis=-1, keepdims=True)
            a_ref[i:i + 1, sl] = o * lax.rsqrt(ms + EPS) * ng_ref[:, sl] * gate_ref[i:i + 1, sl]


def _sb_weights(z2, biases, carries, mask, neg_from):
    tq = z2.shape[0] // 2
    new_carries, weights = [], []
    for h in range(2):
        z = z2[h * tq:(h + 1) * tq] + biases[h]
        sp = _softplus(z)
        if mask is not None:
            sp = jnp.where(mask, sp, 0.0)
        tail = _dot(sp.astype(BF16), neg_from)
        a = jnp.exp(z + tail + carries[h])
        if mask is not None:
            a = jnp.where(mask, a, 0.0)
        weights.append(a.astype(BF16))
        new_carries.append(carries[h] - jnp.sum(sp, axis=-1, keepdims=True))
    return tuple(new_carries), tuple(weights)


def _decode_chunk(qbd, bias, k_pages, v_pages, carry, acc, neg_from):
    n = len(k_pages)

    def lanes(x, i):
        return x[:, i * PAGE:(i + 1) * PAGE]

    k_all = jnp.concatenate([kp[...].astype(BF16) for kp in k_pages], axis=1)
    z = _dot(qbd, k_all) + bias
    sp = _softplus(z)
    sp_rows = jnp.concatenate([lanes(sp, i) for i in range(n)], axis=0)
    hi, lo = _split2(sp_rows)
    tail = _dot(hi, neg_from) + _dot(lo, neg_from)
    tot = jnp.sum(sp_rows, axis=-1, keepdims=True)
    weights = []
    for i in range(n):
        rows = slice(i * SB_HEADS, (i + 1) * SB_HEADS)
        weights.append(jnp.exp(lanes(z, i) + tail[rows] + carry))
        carry = carry - tot[rows]
    new_acc = []
    for h in range(SB_HEADS):
        rows = slice(h * SB_DH, (h + 1) * SB_DH)
        t = acc[h]
        for i in range(n):
            t = t + v_pages[i][rows, :] * jnp.broadcast_to(weights[i][h:h + 1, :], (SB_DH, PAGE))
        new_acc.append(t)
    return carry, new_acc


def _sb_kernel(bias_ref, pt_ref,
               q_ref, k_ref, v_ref, gate_ref,
               qd_ref, gated_ref, biasd_ref, ck_ref, cv_ref,
               o_ref, od_ref,
               kb_ref, v0_ref, v1_ref, kmb_ref, vm0_ref, vm1_ref,
               kbuf, vbuf, sem, acc_ref, carry_ref):
    p = pl.program_id(1)
    qi = pl.program_id(2)
    step = (pl.program_id(0) * pl.num_programs(1) + p) * pl.num_programs(2) + qi
    n_steps = pl.num_programs(0) * pl.num_programs(1) * pl.num_programs(2)
    step_pages = kbuf.shape[1]
    n_pages = pt_ref.shape[1]
    steps_per_sample = n_pages // step_pages

    def page_copies(st, slot):
        sample = st // steps_per_sample
        first = (st % steps_per_sample) * step_pages
        copies = []
        for i in range(step_pages):
            page = pt_ref[sample, n_pages - 1 - (first + i)]
            copies.append(pltpu.make_async_copy(ck_ref.at[page], kbuf.at[slot, i], sem.at[0, slot]))
            copies.append(pltpu.make_async_copy(cv_ref.at[page], vbuf.at[slot, i], sem.at[1, slot]))
        return copies

    slot = step % 2

    @pl.when(step == 0)
    def _():
        for c in page_copies(step, slot):
            c.start()

    lane = lax.broadcasted_iota(jnp.int32, (1, 2 * SB_DH), 1)
    head0 = lane < SB_DH

    @pl.when(qi == 0)
    def _():
        kb_ref[...] = k_ref[0, N_META:, :].astype(BF16)
        v = v_ref[0, N_META:, :]
        v0_ref[...] = jnp.where(head0, v, 0.0).astype(BF16)
        v1_ref[...] = jnp.where(head0, 0.0, v).astype(BF16)
        for ref in (kmb_ref, vm0_ref, vm1_ref):
            ref[...] = jnp.zeros_like(ref)
        kmb_ref[0:N_META, :] = k_ref[0, 0:N_META, :].astype(BF16)
        vm = v_ref[0, 0:N_META, :]
        vm0_ref[0:N_META, :] = jnp.where(head0, vm, 0.0).astype(BF16)
        vm1_ref[0:N_META, :] = jnp.where(head0, 0.0, vm).astype(BF16)

    q = q_ref[0] * (SB_DH ** -0.5)
    q2 = jnp.concatenate([jnp.where(head0, q, 0.0), jnp.where(head0, 0.0, q)], axis=0).astype(BF16)
    biases = (bias_ref[2 * p], bias_ref[2 * p + 1])
    r = lax.broadcasted_iota(jnp.int32, (SB_BLK, SB_BLK), 0)
    s = lax.broadcasted_iota(jnp.int32, (SB_BLK, SB_BLK), 1)
    neg_from = jnp.where(r >= s, -1.0, 0.0).astype(BF16)
    diag_mask = s < r
    rm = lax.broadcasted_iota(jnp.int32, (PAGE, PAGE), 0)
    sm = lax.broadcasted_iota(jnp.int32, (PAGE, PAGE), 1)
    neg_from_m = jnp.where(rm >= sm, -1.0, 0.0).astype(BF16)
    meta_mask = lax.broadcasted_iota(jnp.int32, (SB_BLK, PAGE), 1) < N_META

    def scores(blk):
        return _dot_nt(q2, kb_ref[pl.ds(pl.multiple_of(blk * SB_BLK, SB_BLK), SB_BLK), :])

    def weighted_values(acc, weights, blk):
        rows = pl.ds(pl.multiple_of(blk * SB_BLK, SB_BLK), SB_BLK)
        return acc + _dot(weights[0], v0_ref[rows, :]) + _dot(weights[1], v1_ref[rows, :])

    ahead = jnp.where(step + 1 < n_steps, step + 1, 0)
    for c in page_copies(ahead, 1 - slot):
        c.start()

    zero_c = jnp.zeros((SB_BLK, 1), F32)
    z_meta = _dot_nt(q2, kmb_ref[...])
    carries, weights = _sb_weights(scores(qi), biases, (zero_c, zero_c), diag_mask, neg_from)
    z_next = scores(jnp.maximum(qi - 1, 0))

    def body(n, state):
        carries, acc, z_cur, w_prev = state
        acc = weighted_values(acc, w_prev, qi - n)
        z_nxt = scores(jnp.maximum(qi - 2 - n, 0))
        carries, w_cur = _sb_weights(z_cur, biases, carries, None, neg_from)
        return carries, acc, z_nxt, w_cur

    carries, acc, _, weights = lax.fori_loop(
        0, qi, body, (carries, jnp.zeros((SB_BLK, 2 * SB_DH), F32), z_next, weights))
    acc = weighted_values(acc, weights, 0)

    for c in page_copies(step, slot):
        c.wait()
    part = step % steps_per_sample

    @pl.when(part == 0)
    def _():
        acc_ref[...] = jnp.zeros_like(acc_ref)
        carry_ref[...] = jnp.zeros_like(carry_ref)

    width = SB_HEADS * SB_DH
    hrow = lax.broadcasted_iota(jnp.int32, (SB_HEADS, width), 0)
    hlane = lax.broadcasted_iota(jnp.int32, (SB_HEADS, width), 1) // SB_DH
    own = hrow == hlane
    qd = jnp.broadcast_to(qd_ref[0] * (SB_DH ** -0.5), (SB_HEADS, width))
    qbd = jnp.where(own, qd, 0.0).astype(BF16)
    carry_d = carry_ref[...]
    acc_d = [acc_ref[h * SB_DH:(h + 1) * SB_DH, :] for h in range(SB_HEADS)]
    for c0 in range(0, step_pages, DEC_PAGES):
        idx = range(c0, c0 + DEC_PAGES)
        carry_d, acc_d = _decode_chunk(qbd, biasd_ref[...], [kbuf.at[slot, i] for i in idx],
                                       [vbuf.at[slot, i] for i in idx], carry_d, acc_d, neg_from_m)
    for h in range(SB_HEADS):
        acc_ref[h * SB_DH:(h + 1) * SB_DH, :] = acc_d[h]
    carry_ref[...] = carry_d

    @pl.when(part == steps_per_sample - 1)
    def _():
        od = jnp.sum(acc_ref[...].T, axis=0, keepdims=True)
        od_ref[0] = od * gated_ref[0]

    _, w_meta = _sb_weights(z_meta, biases, carries, meta_mask, neg_from_m)
    acc = acc + _dot(w_meta[0], vm0_ref[...]) + _dot(w_meta[1], vm1_ref[...])
    o_ref[0] = acc * gate_ref[0]

    @pl.when(step == n_steps - 1)
    def _():
        for c in page_copies(ahead, 1 - slot):
            c.wait()


def _merge_kernel(x_ref, a_ref, b_ref, w_ref, y_ref):
    half = a_ref.shape[-1]
    y = _dot(a_ref[...].astype(BF16), w_ref[0:half, :])
    y = y + _dot(b_ref[...].astype(BF16), w_ref[half:2 * half, :])
    y_ref[...] = x_ref[...] + y


def _merge(x2d, a, b, w_bf16, tm):
    rows, d = x2d.shape
    half = a.shape[-1]
    return pl.pallas_call(
        _merge_kernel,
        grid=(rows // tm,),
        in_specs=[
            pl.BlockSpec((tm, d), lambda i: (i, 0)),
            pl.BlockSpec((tm, half), lambda i: (i, 0)),
            pl.BlockSpec((tm, half), lambda i: (i, 0)),
            pl.BlockSpec(w_bf16.shape, lambda i: (0, 0)),
        ],
        out_specs=pl.BlockSpec((tm, d), lambda i: (i, 0)),
        out_shape=jax.ShapeDtypeStruct((rows, d), F32),
        compiler_params=pltpu.CompilerParams(
            dimension_semantics=("parallel",), vmem_limit_bytes=VMEM_LIMIT),
        name="merge",
    )(x2d, a, b, w_bf16)


def kernel(x_prompt, x_sample, cache_k, cache_v, state_hgrn, page_table, meta_tokens, norm_g,
           w_in, lb_logits, hg_norm_g, q_norm_g, k_norm_g, sb_bias, w_out):
    nb, seq, d = x_prompt.shape
    db = x_sample.shape[0]
    n_pages = page_table.shape[1]
    n_phys = cache_k.shape[1]
    assert w_in.shape[0] == 1 and w_in.shape[2] == N_SEG * SEG
    assert seq % SB_BLK == 0 and seq % HG_CHUNK == 0 and nb % HG_BATCH == 0
    assert db % STEP_SAMPLES == 0 and seq % PROJ_ROWS == 0

    w_in_b = w_in[0].astype(BF16)
    w_out_b = w_out[0].astype(BF16)
    g_row = norm_g[0][None, :]
    qg = jnp.tile(q_norm_g[0], SB_HEADS)[None, :]
    kg = jnp.tile(k_norm_g[0], SB_HEADS)[None, :]
    ng = hg_norm_g[0].reshape(1, HG_HEADS * HG_DIM)
    lane_head = jnp.arange(MXU_TILE) // SB_DH
    grp = (lane_head[:, None] == lane_head[None, :]).astype(BF16)
    bias = sb_bias[0].astype(F32)

    xs2d = x_sample.reshape(db, d)
    small = jnp.concatenate([xs2d, meta_tokens.astype(F32)], axis=0)
    n_small = small.shape[0]
    s_out = _project(small[None], n_small, g_row, w_in_b, lb_logits, qg, kg, grp)
    hq_s, lf_s, hv_s, hgate_s, sq_s, sk_s, sv_s, sgate_s = [a[0, :db] for a in s_out]
    _, lf_m, hv_m, _, _, sk_m, sv_m, _ = [a[0, db:] for a in s_out]
    hq, lf, hv, hgate, sq, sk_all, sv_all, sgate = _project(
        x_prompt, PROJ_ROWS, g_row, w_in_b, lb_logits, qg, kg, grp, lead_k=sk_m, lead_v=sv_m)

    pad_m = ((0, HG_CHUNK - N_META), (0, 0))
    st_meta = pl.pallas_call(
        _hgrn_meta_kernel,
        out_shape=jax.ShapeDtypeStruct((HG_HEADS, HG_DIM, HG_DIM), F32),
        name="hgrn_meta",
    )(jnp.pad(lf_m, pad_m), jnp.pad(hv_m, pad_m), _hgrn_sum_matrix(HG_CHUNK, False))

    rspec = pl.BlockSpec((STEP_SAMPLES, SEG), lambda i: (i, 0))
    sspec = pl.BlockSpec((STEP_SAMPLES, HG_HEADS, HG_DIM, HG_DIM), lambda i: (i, 0, 0, 0))
    s_sample, a_hg_s = pl.pallas_call(
        _hgrn_step_kernel,
        grid=(db // STEP_SAMPLES,),
        in_specs=[rspec, rspec, rspec, rspec, pl.BlockSpec((1, SEG), lambda i: (0, 0)), sspec],
        out_specs=[sspec, rspec],
        out_shape=[jax.ShapeDtypeStruct((db, HG_HEADS, HG_DIM, HG_DIM), F32),
                   jax.ShapeDtypeStruct((db, SEG), F32)],
        compiler_params=pltpu.CompilerParams(
            dimension_semantics=("parallel",), vmem_limit_bytes=VMEM_LIMIT),
        name="hgrn_step",
    )(hq_s, lf_s, hv_s, hgate_s, ng, state_hgrn[0])

    n_qb = seq // SB_BLK
    n_pairs = SB_HEADS // 2
    n_steps = nb * n_pairs * n_qb
    assert (db * n_pages) % n_steps == 0
    step_pages = db * n_pages // n_steps
    assert n_pages % step_pages == 0 and step_pages % DEC_PAGES == 0
    steps_per_sample = n_pages // step_pages
    ck = jnp.transpose(cache_k[0], (0, 2, 3, 1)).reshape(n_phys, SEG, PAGE)
    cv = jnp.transpose(cache_v[0], (0, 2, 3, 1)).reshape(n_phys, SEG, PAGE)
    bias_bc = jnp.broadcast_to(bias[:, None], (SB_HEADS, DEC_PAGES * PAGE))

    qspec = pl.BlockSpec((1, SB_BLK, 2 * SB_DH), lambda b, p, i, *_: (b, i, p))
    kvspec = pl.BlockSpec((1, N_META + seq, 2 * SB_DH), lambda b, p, i, *_: (b, 0, p))
    vec_spec = pl.BlockSpec(
        (1, 1, SEG), lambda b, p, i, *_: (((b * n_pairs + p) * n_qb + i) // steps_per_sample, 0, 0))
    hbm_spec = pl.BlockSpec(memory_space=pl.ANY)
    b_sb, b_sb_s = pl.pallas_call(
        _sb_kernel,
        grid_spec=pltpu.PrefetchScalarGridSpec(
            num_scalar_prefetch=2,
            grid=(nb, n_pairs, n_qb),
            in_specs=[qspec, kvspec, kvspec, qspec,
                      vec_spec, vec_spec,
                      pl.BlockSpec((SB_HEADS, DEC_PAGES * PAGE), lambda b, p, i, *_: (0, 0)),
                      hbm_spec, hbm_spec],
            out_specs=[qspec, vec_spec],
            scratch_shapes=[pltpu.VMEM((seq, 2 * SB_DH), BF16)] * 3
                           + [pltpu.VMEM((PAGE, 2 * SB_DH), BF16)] * 3
                           + [pltpu.VMEM((2, step_pages, SEG, PAGE), F32)] * 2
                           + [pltpu.SemaphoreType.DMA((2, 2)),
                              pltpu.VMEM((SEG, PAGE), F32), pltpu.VMEM((SB_HEADS, PAGE), F32)],
        ),
        out_shape=[jax.ShapeDtypeStruct((nb, seq, SEG), F32),
                   jax.ShapeDtypeStruct((db, 1, SEG), F32)],
        compiler_params=pltpu.CompilerParams(
            dimension_semantics=("arbitrary", "arbitrary", "arbitrary"),
            vmem_limit_bytes=SB_VMEM_LIMIT),
        name="sb_attention",
    )(bias, page_table,
      sq, sk_all, sv_all, sgate,
      sq_s.reshape(db, 1, SEG), sgate_s.reshape(db, 1, SEG), bias_bc, ck, cv)

    n_chunks = seq // HG_CHUNK
    sums = _hgrn_sum_matrix(HG_CHUNK, True)
    lvl = _hgrn_level_index(HG_CHUNK)
    cspec = pl.BlockSpec((HG_BATCH, HG_CHUNK, SEG), lambda b, c: (b, c, 0))
    xspec = pl.BlockSpec((HG_BATCH, HG_CHUNK, d), lambda b, c: (b, c, 0))
    y_prompt, s_prompt = pl.pallas_call(
        _hgrn_prompt_kernel,
        grid=(nb // HG_BATCH, n_chunks),
        in_specs=[cspec, cspec, cspec, cspec,
                  pl.BlockSpec((1, SEG), lambda b, c: (0, 0)),
                  pl.BlockSpec((HG_HEADS, HG_DIM, HG_DIM), lambda b, c: (0, 0, 0)),
                  pl.BlockSpec(sums.shape, lambda b, c: (0, 0)),
                  pl.BlockSpec(lvl.shape, lambda b, c: (0, 0)),
                  xspec, cspec,
                  pl.BlockSpec(w_out_b.shape, lambda b, c: (0, 0))],
        out_specs=[xspec,
                   pl.BlockSpec((HG_BATCH, HG_HEADS, HG_DIM, HG_DIM), lambda b, c: (b, 0, 0, 0))],
        out_shape=[jax.ShapeDtypeStruct((nb, seq, d), F32),
                   jax.ShapeDtypeStruct((nb, HG_HEADS, HG_DIM, HG_DIM), F32)],
        scratch_shapes=[pltpu.VMEM((HG_BATCH, HG_HEADS, HG_DIM, HG_DIM), F32)],
        compiler_params=pltpu.CompilerParams(
            dimension_semantics=("parallel", "arbitrary"), vmem_limit_bytes=VMEM_LIMIT),
        name="hgrn_prompt",
    )(hq, lf, hv, hgate, ng, st_meta, sums, lvl, x_prompt, b_sb, w_out_b)

    y_sample = _merge(xs2d, a_hg_s, b_sb_s.reshape(db, SEG), w_out_b, db)

    return (y_prompt,
            y_sample.reshape(db, 1, d),
            sk_all.reshape(1, nb, N_META + seq, SB_HEADS, SB_DH),
            sv_all.reshape(1, nb, N_META + seq, SB_HEADS, SB_DH),
            s_prompt[None],
            sk_s.reshape(1, db, 1, SB_HEADS, SB_DH),
            sv_s.reshape(1, db, 1, SB_HEADS, SB_DH),
            s_sample[None])
```

```python
import functools

import jax
import jax.numpy as jnp
import numpy as np
from jax import lax
from jax.experimental import pallas as pl
from jax.experimental.pallas import tpu as pltpu

F32 = jnp.float32
BF16 = jnp.bfloat16
EPS = 1e-6

SEG = 512
N_SEG = 8
HG_HEADS = 4
HG_DIM = 128
SB_HEADS = 8
SB_DH = 64
PAGE = 128
MXU_TILE = 256
SUBLANES = 8
N_META = 16

HG_CHUNK = 128
HG_SMALL_LEVELS = (4, 2, 1)
HG_BATCH = 2
SB_BLK = 256
PROJ_ROWS = 512
DEC_PAGES = 32
STEP_SAMPLES = 8

VMEM_LIMIT = 56 * 1024 * 1024
SB_VMEM_LIMIT = 56 * 1024 * 1024

NT_DIMS = (((1,), (1,)), ((), ()))


def _dot(a, b):
    return jnp.dot(a, b, preferred_element_type=F32)


def _dot_nt(a, b):
    return lax.dot_general(a, b, NT_DIMS, preferred_element_type=F32)


def _split2(x):
    hi = x.astype(BF16)
    lo = (x - hi.astype(F32)).astype(BF16)
    return hi, lo


def _split3(x):
    hi = x.astype(BF16)
    r = x - hi.astype(F32)
    mid = r.astype(BF16)
    lo = (r - mid.astype(F32)).astype(BF16)
    return hi, mid, lo


def _softplus(z):
    return jnp.maximum(z, 0.0) + jnp.log(1.0 + jnp.exp(-jnp.abs(z)))


def _silu(z):
    return z * (1.0 / (1.0 + jnp.exp(-z)))


def _proj_kernel(x_ref, g_ref, w_ref, lbl_ref, qg_ref, kg_ref, grp_ref,
                 *rest, lead):
    lead_refs = rest[:-N_SEG]
    hq_ref, lf_ref, hv_ref, hgate_ref, sq_ref, sk_ref, sv_ref, sgate_ref = rest[-N_SEG:]
    ti = pl.program_id(1)
    tm = x_ref.shape[1]
    x = x_ref[0]
    ms = jnp.mean(x * x, axis=-1, keepdims=True)
    h = (x * lax.rsqrt(ms + EPS) * g_ref[...]).astype(BF16)

    def seg(j):
        return _dot(h, w_ref[:, j * SEG:(j + 1) * SEG])

    def group_rms(u, gain):
        hi, lo = _split2(u * u)
        gw = grp_ref.shape[0]
        ss = jnp.concatenate(
            [_dot(hi[:, c:c + gw], grp_ref[...]) + _dot(lo[:, c:c + gw], grp_ref[...])
             for c in range(0, SEG, gw)], axis=1)
        return u * lax.rsqrt(ss * (1.0 / SB_DH) + EPS) * gain

    hq_ref[0] = _silu(seg(0))

    l = lbl_ref[...]
    e = jnp.exp(l - jnp.max(l, axis=0, keepdims=True))
    lb = e[0:1, :] / jnp.sum(e, axis=0, keepdims=True)
    a = jnp.log(lb)
    z = seg(1)
    bb = jnp.log1p(-lb) + (jnp.minimum(z, 0.0) - jnp.log(1.0 + jnp.exp(-jnp.abs(z))))
    lf_ref[0] = jnp.maximum(a, bb) + jnp.log(1.0 + jnp.exp(-jnp.abs(a - bb)))

    hv_ref[0] = seg(2)
    hgate_ref[0] = _silu(seg(3))
    sq_ref[0] = group_rms(seg(4), qg_ref[...])
    sgate_ref[0] = _silu(seg(7))

    rows = pl.ds(pl.multiple_of(lead + ti * tm, SUBLANES), tm)
    sk_ref[0, rows, :] = group_rms(seg(5), kg_ref[...])
    sv_ref[0, rows, :] = seg(6)
    if lead:
        @pl.when(ti == 0)
        def _():
            sk_ref[0, 0:lead, :] = lead_refs[0][...]
            sv_ref[0, 0:lead, :] = lead_refs[1][...]


def _project(x3d, tm, norm_g, w_bf16, lb_logits, qg, kg, grp, lead_k=None, lead_v=None):
    n_seq, rows, d = x3d.shape
    lead = 0 if lead_k is None else lead_k.shape[0]
    lead_args = [] if lead_k is None else [lead_k, lead_v]
    const = lambda s, t: (0, 0)
    row_spec = pl.BlockSpec((1, tm, SEG), lambda s, t: (s, t, 0))
    row_shape = jax.ShapeDtypeStruct((n_seq, rows, SEG), F32)
    kv_spec = pl.BlockSpec((1, lead + rows, SEG), lambda s, t: (s, 0, 0))
    kv_shape = jax.ShapeDtypeStruct((n_seq, lead + rows, SEG), F32)
    return pl.pallas_call(
        functools.partial(_proj_kernel, lead=lead),
        grid=(n_seq, rows // tm),
        in_specs=[
            pl.BlockSpec((1, tm, d), lambda s, t: (s, t, 0)),
            pl.BlockSpec((1, d), const),
            pl.BlockSpec(w_bf16.shape, const),
            pl.BlockSpec(lb_logits.shape, const),
            pl.BlockSpec((1, SEG), const),
            pl.BlockSpec((1, SEG), const),
            pl.BlockSpec(grp.shape, const),
        ] + [pl.BlockSpec((lead, SEG), const)] * len(lead_args),
        out_specs=[row_spec] * 5 + [kv_spec, kv_spec, row_spec],
        out_shape=[row_shape] * 5 + [kv_shape, kv_shape, row_shape],
        compiler_params=pltpu.CompilerParams(
            dimension_semantics=("parallel", "arbitrary"), vmem_limit_bytes=VMEM_LIMIT),
        name="proj",
    )(x3d, norm_g, w_bf16, lb_logits, qg, kg, grp, *lead_args)


def _hgrn_sum_matrix(c, with_small_levels):
    t = np.arange(c)[:, None]
    j = np.arange(c)[None, :]
    blocks = [j <= t]
    if with_small_levels:
        for m in HG_SMALL_LEVELS[:-1]:
            f = (t // m) * m
            blocks.append((j > f) & (j <= t))
            blocks.append((j > t) & (j <= f + m))
        blocks.append(j == t + 1)
    return jnp.asarray(np.concatenate(blocks, axis=0), dtype=BF16)


def _hgrn_level_index(c):
    t = np.arange(c)[:, None]
    s = np.arange(c)[None, :]
    x = t ^ s
    lvl = np.where(x > 0, 2 ** np.floor(np.log2(np.maximum(x, 1))).astype(np.int64), 0)
    return jnp.asarray(np.where(s > t, -1, lvl), dtype=jnp.int32)


def _hgrn_chunk(q, lf, v, st, sums, lvl):
    c = q.shape[0]
    width = q.shape[1]
    want_out = lvl is not None
    summed = _dot(jnp.concatenate([sums, sums], axis=1), jnp.concatenate(_split2(lf), axis=0))

    def block(i):
        return summed[i * c:(i + 1) * c, :]

    b = block(0)
    kk = 1.0 - jnp.exp(lf)
    b_last = b[c - 1:c, :]
    kdec = (kk * jnp.exp(jnp.minimum(b_last - b, 0.0))).astype(BF16)
    sdec = jnp.exp(b_last)

    new_st = []
    for h in range(HG_HEADS):
        sl = slice(h * HG_DIM, (h + 1) * HG_DIM)
        vt = v[:, sl].T.astype(BF16)
        new_st.append(st[h] * sdec[:, sl] + _dot(vt, kdec[:, sl]))
    if not want_out:
        return new_st, None

    levels = [(0, q, kk)]
    m = c // 2
    while m > HG_SMALL_LEVELS[0]:
        nb = c // m
        starts = [b[i * m:i * m + 1, :] for i in range(nb)]
        hold = jnp.concatenate([jnp.broadcast_to(starts[i], (m, width)) for i in range(nb)], axis=0)
        nxt = jnp.concatenate(
            [jnp.broadcast_to(starts[min(i + 1, nb - 1)], (m, width)) for i in range(nb)], axis=0)
        levels.append((m, q * jnp.exp(b - hold), kk * jnp.exp(jnp.minimum(nxt - b, 0.0))))
        m //= 2
    for i, m in enumerate(HG_SMALL_LEVELS[:-1]):
        levels.append((m, q * jnp.exp(block(1 + 2 * i)), kk * jnp.exp(block(2 + 2 * i))))
    levels.append((1, q, kk * jnp.exp(block(2 * len(HG_SMALL_LEVELS) - 1))))

    qdec = (q * jnp.exp(b)).astype(BF16)
    levels = [(m, qm.astype(BF16), km) for m, qm, km in levels]

    assert c == HG_DIM
    zero = jnp.zeros((HG_DIM, HG_DIM), BF16)

    def blockdiag(y0, y1):
        return jnp.concatenate([jnp.concatenate([y0, zero], axis=1),
                                jnp.concatenate([zero, y1], axis=1)], axis=0)

    lvl2 = jnp.concatenate([lvl, lvl], axis=1)
    outs = []
    for h0 in range(0, HG_HEADS, 2):
        s0 = slice(h0 * HG_DIM, (h0 + 1) * HG_DIM)
        s1 = slice((h0 + 1) * HG_DIM, (h0 + 2) * HG_DIM)
        pair = slice(h0 * HG_DIM, (h0 + 2) * HG_DIM)
        a = jnp.zeros((c, 2 * c), F32)
        for m, qm, km in levels:
            kt = blockdiag(km[:, s0].T.astype(BF16), km[:, s1].T.astype(BF16))
            a = jnp.where(lvl2 == m, _dot(qm[:, pair], kt), a)
        o = _dot_nt(qdec[:, pair], blockdiag(st[h0].astype(BF16), st[h0 + 1].astype(BF16)))
        o = o + _dot(a.astype(BF16), blockdiag(v[:, s0].astype(BF16), v[:, s1].astype(BF16)))
        outs.extend([o[:, 0:HG_DIM], o[:, HG_DIM:2 * HG_DIM]])
    return new_st, outs


def _hgrn_meta_kernel(lf_ref, v_ref, sums_ref, st_ref):
    st0 = [jnp.zeros((HG_DIM, HG_DIM), F32)] * HG_HEADS
    lf = lf_ref[...]
    new_st, _ = _hgrn_chunk(lf, lf, v_ref[...], st0, sums_ref[...], None)
    for h in range(HG_HEADS):
        st_ref[h] = new_st[h]


def _hgrn_prompt_kernel(q_ref, lf_ref, v_ref, gate_ref, ng_ref, st0_ref, sums_ref, lvl_ref,
                        x_ref, sb_ref, w_ref, y_ref, s_ref, st_ref):
    ci = pl.program_id(1)
    n_batch = q_ref.shape[0]

    @pl.when(ci == 0)
    def _():
        for i in range(n_batch):
            st_ref[i] = st0_ref[...]

    final = []
    for i in range(n_batch):
        st = [st_ref[i, h] for h in range(HG_HEADS)]
        new_st, outs = _hgrn_chunk(q_ref[i], lf_ref[i], v_ref[i], st, sums_ref[...], lvl_ref[...])
        final.append(new_st)
        gated = []
        for h in range(HG_HEADS):
            sl = slice(h * HG_DIM, (h + 1) * HG_DIM)
            st_ref[i, h] = new_st[h]
            o = outs[h]
            ms = jnp.mean(o * o, axis=-1, keepdims=True)
            gated.append((o * lax.rsqrt(ms + EPS) * ng_ref[:, sl] * gate_ref[i, :, sl]).astype(BF16))
        a = jnp.concatenate(gated, axis=1)
        half = a.shape[1]
        y = _dot(a, w_ref[0:half, :]) + _dot(sb_ref[i].astype(BF16), w_ref[half:2 * half, :])
        y_ref[i] = x_ref[i] + y

    @pl.when(ci == pl.num_programs(1) - 1)
    def _():
        for i in range(n_batch):
            for h in range(HG_HEADS):
                s_ref[i, h] = final[i][h].T


def _hgrn_step_kernel(q_ref, lf_ref, v_ref, gate_ref, ng_ref, s0_ref, s_ref, a_ref):
    for i in range(STEP_SAMPLES):
        q = q_ref[i:i + 1, :]
        f = jnp.exp(lf_ref[i:i + 1, :])
        v = v_ref[i:i + 1, :]
        qk = q * (1.0 - f)
        for h in range(HG_HEADS):
            sl = slice(h * HG_DIM, (h + 1) * HG_DIM)
            fc = jnp.broadcast_to(f[:, sl], (HG_DIM, HG_DIM)).T
            sn = fc * s0_ref[i, h] + (1.0 - fc) * v[:, sl]
            s_ref[i, h] = sn
            qf = jnp.broadcast_to(q[:, sl] * f[:, sl], (8, HG_DIM)).astype(BF16)
            o = _dot(qf, s0_ref[i, h].astype(BF16))[0:1, :]
            o = o + jnp.sum(qk[:, sl], axis=-1, keepdims=True) * v[:, sl]
            ms = jnp.mean(o * o, axis=-1, keepdims=True)
            a_ref[i:i + 1, sl] = o * lax.rsqrt(ms + EPS) * ng_ref[:, sl] * gate_ref[i:i + 1, sl]


def _sb_weights(z2, biases, carries, mask, neg_from):
    tq = z2.shape[0] // 2
    new_carries, weights = [], []
    for h in range(2):
        z = z2[h * tq:(h + 1) * tq] + biases[h]
        sp = _softplus(z)
        if mask is not None:
            sp = jnp.where(mask, sp, 0.0)
        tail = _dot(sp.astype(BF16), neg_from)
        a = jnp.exp(z + tail + carries[h])
        if mask is not None:
            a = jnp.where(mask, a, 0.0)
        weights.append(a.astype(BF16))
        new_carries.append(carries[h] - jnp.sum(sp, axis=-1, keepdims=True))
    return tuple(new_carries), tuple(weights)


def _decode_chunk(qbd, bias, k_pages, v_pages, carry, acc, neg_from):
    n = len(k_pages)

    def lanes(x, i):
        return x[:, i * PAGE:(i + 1) * PAGE]

    k_all = jnp.concatenate([kp[...].astype(BF16) for kp in k_pages], axis=1)
    z = _dot(qbd, k_all) + bias
    sp = _softplus(z)
    sp_rows = jnp.concatenate([lanes(sp, i) for i in range(n)], axis=0)
    hi, lo = _split2(sp_rows)
    tail = _dot(hi, neg_from) + _dot(lo, neg_from)
    tot = jnp.sum(sp_rows, axis=-1, keepdims=True)
    weights = []
    for i in range(n):
        rows = slice(i * SB_HEADS, (i + 1) * SB_HEADS)
        weights.append(jnp.exp(lanes(z, i) + tail[rows] + carry))
        carry = carry - tot[rows]
    new_acc = []
    for h in range(SB_HEADS):
        rows = slice(h * SB_DH, (h + 1) * SB_DH)
        t = acc[h]
        for i in range(n):
            t = t + v_pages[i][rows, :] * jnp.broadcast_to(weights[i][h:h + 1, :], (SB_DH, PAGE))
        new_acc.append(t)
    return carry, new_acc


def _sb_kernel(bias_ref, pt_ref,
               q_ref, k_ref, v_ref, gate_ref,
               qd_ref, gated_ref, biasd_ref, ck_ref, cv_ref,
               o_ref, od_ref,
               kb_ref, v0_ref, v1_ref, kmb_ref, vm0_ref, vm1_ref,
               kbuf, vbuf, sem, acc_ref, carry_ref):
    p = pl.program_id(1)
    qi = pl.program_id(2)
    step = (pl.program_id(0) * pl.num_programs(1) + p) * pl.num_programs(2) + qi
    n_steps = pl.num_programs(0) * pl.num_programs(1) * pl.num_programs(2)
    step_pages = kbuf.shape[1]
    n_pages = pt_ref.shape[1]
    steps_per_sample = n_pages // step_pages

    def page_copies(st, slot):
        sample = st // steps_per_sample
        first = (st % steps_per_sample) * step_pages
        copies = []
        for i in range(step_pages):
            page = pt_ref[sample, n_pages - 1 - (first + i)]
            copies.append(pltpu.make_async_copy(ck_ref.at[page], kbuf.at[slot, i], sem.at[0, slot]))
            copies.append(pltpu.make_async_copy(cv_ref.at[page], vbuf.at[slot, i], sem.at[1, slot]))
        return copies

    slot = step % 2

    @pl.when(step == 0)
    def _():
        for c in page_copies(step, slot):
            c.start()

    lane = lax.broadcasted_iota(jnp.int32, (1, 2 * SB_DH), 1)
    head0 = lane < SB_DH

    @pl.when(qi == 0)
    def _():
        kb_ref[...] = k_ref[0, N_META:, :].astype(BF16)
        v = v_ref[0, N_META:, :]
        v0_ref[...] = jnp.where(head0, v, 0.0).astype(BF16)
        v1_ref[...] = jnp.where(head0, 0.0, v).astype(BF16)
        for ref in (kmb_ref, vm0_ref, vm1_ref):
            ref[...] = jnp.zeros_like(ref)
        kmb_ref[0:N_META, :] = k_ref[0, 0:N_META, :].astype(BF16)
        vm = v_ref[0, 0:N_META, :]
        vm0_ref[0:N_META, :] = jnp.where(head0, vm, 0.0).astype(BF16)
        vm1_ref[0:N_META, :] = jnp.where(head0, 0.0, vm).astype(BF16)

    q = q_ref[0] * (SB_DH ** -0.5)
    q2 = jnp.concatenate([jnp.where(head0, q, 0.0), jnp.where(head0, 0.0, q)], axis=0).astype(BF16)
    biases = (bias_ref[2 * p], bias_ref[2 * p + 1])
    r = lax.broadcasted_iota(jnp.int32, (SB_BLK, SB_BLK), 0)
    s = lax.broadcasted_iota(jnp.int32, (SB_BLK, SB_BLK), 1)
    neg_from = jnp.where(r >= s, -1.0, 0.0).astype(BF16)
    diag_mask = s < r
    rm = lax.broadcasted_iota(jnp.int32, (PAGE, PAGE), 0)
    sm = lax.broadcasted_iota(jnp.int32, (PAGE, PAGE), 1)
    neg_from_m = jnp.where(rm >= sm, -1.0, 0.0).astype(BF16)
    meta_mask = lax.broadcasted_iota(jnp.int32, (SB_BLK, PAGE), 1) < N_META

    def scores(blk):
        return _dot_nt(q2, kb_ref[pl.ds(pl.multiple_of(blk * SB_BLK, SB_BLK), SB_BLK), :])

    def weighted_values(acc, weights, blk):
        rows = pl.ds(pl.multiple_of(blk * SB_BLK, SB_BLK), SB_BLK)
        return acc + _dot(weights[0], v0_ref[rows, :]) + _dot(weights[1], v1_ref[rows, :])

    ahead = jnp.where(step + 1 < n_steps, step + 1, 0)
    for c in page_copies(ahead, 1 - slot):
        c.start()

    zero_c = jnp.zeros((SB_BLK, 1), F32)
    z_meta = _dot_nt(q2, kmb_ref[...])
    carries, weights = _sb_weights(scores(qi), biases, (zero_c, zero_c), diag_mask, neg_from)
    z_next = scores(jnp.maximum(qi - 1, 0))

    def body(n, state):
        carries, acc, z_cur, w_prev = state
        acc = weighted_values(acc, w_prev, qi - n)
        z_nxt = scores(jnp.maximum(qi - 2 - n, 0))
        carries, w_cur = _sb_weights(z_cur, biases, carries, None, neg_from)
        return carries, acc, z_nxt, w_cur

    carries, acc, _, weights = lax.fori_loop(
        0, qi, body, (carries, jnp.zeros((SB_BLK, 2 * SB_DH), F32), z_next, weights))
    acc = weighted_values(acc, weights, 0)

    for c in page_copies(step, slot):
        c.wait()
    part = step % steps_per_sample

    @pl.when(part == 0)
    def _():
        acc_ref[...] = jnp.zeros_like(acc_ref)
        carry_ref[...] = jnp.zeros_like(carry_ref)

    width = SB_HEADS * SB_DH
    hrow = lax.broadcasted_iota(jnp.int32, (SB_HEADS, width), 0)
    hlane = lax.broadcasted_iota(jnp.int32, (SB_HEADS, width), 1) // SB_DH
    own = hrow == hlane
    qd = jnp.broadcast_to(qd_ref[0] * (SB_DH ** -0.5), (SB_HEADS, width))
    qbd = jnp.where(own, qd, 0.0).astype(BF16)
    carry_d = carry_ref[...]
    acc_d = [acc_ref[h * SB_DH:(h + 1) * SB_DH, :] for h in range(SB_HEADS)]
    for c0 in range(0, step_pages, DEC_PAGES):
        idx = range(c0, c0 + DEC_PAGES)
        carry_d, acc_d = _decode_chunk(qbd, biasd_ref[...], [kbuf.at[slot, i] for i in idx],
                                       [vbuf.at[slot, i] for i in idx], carry_d, acc_d, neg_from_m)
    for h in range(SB_HEADS):
        acc_ref[h * SB_DH:(h + 1) * SB_DH, :] = acc_d[h]
    carry_ref[...] = carry_d

    @pl.when(part == steps_per_sample - 1)
    def _():
        od = jnp.sum(acc_ref[...].T, axis=0, keepdims=True)
        od_ref[0] = od * gated_ref[0]

    _, w_meta = _sb_weights(z_meta, biases, carries, meta_mask, neg_from_m)
    acc = acc + _dot(w_meta[0], vm0_ref[...]) + _dot(w_meta[1], vm1_ref[...])
    o_ref[0] = acc * gate_ref[0]

    @pl.when(step == n_steps - 1)
    def _():
        for c in page_copies(ahead, 1 - slot):
            c.wait()


def _merge_kernel(x_ref, a_ref, b_ref, w_ref, y_ref):
    half = a_ref.shape[-1]
    y = _dot(a_ref[...].astype(BF16), w_ref[0:half, :])
    y = y + _dot(b_ref[...].astype(BF16), w_ref[half:2 * half, :])
    y_ref[...] = x_ref[...] + y


def _merge(x2d, a, b, w_bf16, tm):
    rows, d = x2d.shape
    half = a.shape[-1]
    return pl.pallas_call(
        _merge_kernel,
        grid=(rows // tm,),
        in_specs=[
            pl.BlockSpec((tm, d), lambda i: (i, 0)),
            pl.BlockSpec((tm, half), lambda i: (i, 0)),
            pl.BlockSpec((tm, half), lambda i: (i, 0)),
            pl.BlockSpec(w_bf16.shape, lambda i: (0, 0)),
        ],
        out_specs=pl.BlockSpec((tm, d), lambda i: (i, 0)),
        out_shape=jax.ShapeDtypeStruct((rows, d), F32),
        compiler_params=pltpu.CompilerParams(
            dimension_semantics=("parallel",), vmem_limit_bytes=VMEM_LIMIT),
        name="merge",
    )(x2d, a, b, w_bf16)


def kernel(x_prompt, x_sample, cache_k, cache_v, state_hgrn, page_table, meta_tokens, norm_g,
           w_in, lb_logits, hg_norm_g, q_norm_g, k_norm_g, sb_bias, w_out):
    nb, seq, d = x_prompt.shape
    db = x_sample.shape[0]
    n_pages = page_table.shape[1]
    n_phys = cache_k.shape[1]
    assert w_in.shape[0] == 1 and w_in.shape[2] == N_SEG * SEG
    assert seq % SB_BLK == 0 and seq % HG_CHUNK == 0 and nb % HG_BATCH == 0
    assert db % STEP_SAMPLES == 0 and seq % PROJ_ROWS == 0

    w_in_b = w_in[0].astype(BF16)
    w_out_b = w_out[0].astype(BF16)
    g_row = norm_g[0][None, :]
    qg = jnp.tile(q_norm_g[0], SB_HEADS)[None, :]
    kg = jnp.tile(k_norm_g[0], SB_HEADS)[None, :]
    ng = hg_norm_g[0].reshape(1, HG_HEADS * HG_DIM)
    lane_head = jnp.arange(MXU_TILE) // SB_DH
    grp = (lane_head[:, None] == lane_head[None, :]).astype(BF16)
    bias = sb_bias[0].astype(F32)

    xs2d = x_sample.reshape(db, d)
    small = jnp.concatenate([xs2d, meta_tokens.astype(F32)], axis=0)
    n_small = small.shape[0]
    s_out = _project(small[None], n_small, g_row, w_in_b, lb_logits, qg, kg, grp)
    hq_s, lf_s, hv_s, hgate_s, sq_s, sk_s, sv_s, sgate_s = [a[0, :db] for a in s_out]
    _, lf_m, hv_m, _, _, sk_m, sv_m, _ = [a[0, db:] for a in s_out]
    hq, lf, hv, hgate, sq, sk_all, sv_all, sgate = _project(
        x_prompt, PROJ_ROWS, g_row, w_in_b, lb_logits, qg, kg, grp, lead_k=sk_m, lead_v=sv_m)

    pad_m = ((0, HG_CHUNK - N_META), (0, 0))
    st_meta = pl.pallas_call(
        _hgrn_meta_kernel,
        out_shape=jax.ShapeDtypeStruct((HG_HEADS, HG_DIM, HG_DIM), F32),
        name="hgrn_meta",
    )(jnp.pad(lf_m, pad_m), jnp.pad(hv_m, pad_m), _hgrn_sum_matrix(HG_CHUNK, False))

    rspec = pl.BlockSpec((STEP_SAMPLES, SEG), lambda i: (i, 0))
    sspec = pl.BlockSpec((STEP_SAMPLES, HG_HEADS, HG_DIM, HG_DIM), lambda i: (i, 0, 0, 0))
    s_sample, a_hg_s = pl.pallas_call(
        _hgrn_step_kernel,
        grid=(db // STEP_SAMPLES,),
        in_specs=[rspec, rspec, rspec, rspec, pl.BlockSpec((1, SEG), lambda i: (0, 0)), sspec],
        out_specs=[sspec, rspec],
        out_shape=[jax.ShapeDtypeStruct((db, HG_HEADS, HG_DIM, HG_DIM), F32),
                   jax.ShapeDtypeStruct((db, SEG), F32)],
        compiler_params=pltpu.CompilerParams(
            dimension_semantics=("parallel",), vmem_limit_bytes=VMEM_LIMIT),
        name="hgrn_step",
    )(hq_s, lf_s, hv_s, hgate_s, ng, state_hgrn[0])

    n_qb = seq // SB_BLK
    n_pairs = SB_HEADS // 2
    n_steps = nb * n_pairs * n_qb
    assert (db * n_pages) % n_steps == 0
    step_pages = db * n_pages // n_steps
    assert n_pages % step_pages == 0 and step_pages % DEC_PAGES == 0
    steps_per_sample = n_pages // step_pages
    ck = jnp.transpose(cache_k[0], (0, 2, 3, 1)).reshape(n_phys, SEG, PAGE)
    cv = jnp.transpose(cache_v[0], (0, 2, 3, 1)).reshape(n_phys, SEG, PAGE)
    bias_bc = jnp.broadcast_to(bias[:, None], (SB_HEADS, DEC_PAGES * PAGE))

    qspec = pl.BlockSpec((1, SB_BLK, 2 * SB_DH), lambda b, p, i, *_: (b, i, p))
    kvspec = pl.BlockSpec((1, N_META + seq, 2 * SB_DH), lambda b, p, i, *_: (b, 0, p))
    vec_spec = pl.BlockSpec(
        (1, 1, SEG), lambda b, p, i, *_: (((b * n_pairs + p) * n_qb + i) // steps_per_sample, 0, 0))
    hbm_spec = pl.BlockSpec(memory_space=pl.ANY)
    b_sb, b_sb_s = pl.pallas_call(
        _sb_kernel,
        grid_spec=pltpu.PrefetchScalarGridSpec(
            num_scalar_prefetch=2,
            grid=(nb, n_pairs, n_qb),
            in_specs=[qspec, kvspec, kvspec, qspec,
                      vec_spec, vec_spec,
                      pl.BlockSpec((SB_HEADS, DEC_PAGES * PAGE), lambda b, p, i, *_: (0, 0)),
                      hbm_spec, hbm_spec],
            out_specs=[qspec, vec_spec],
            scratch_shapes=[pltpu.VMEM((seq, 2 * SB_DH), BF16)] * 3
                           + [pltpu.VMEM((PAGE, 2 * SB_DH), BF16)] * 3
                           + [pltpu.VMEM((2, step_pages, SEG, PAGE), F32)] * 2
                           + [pltpu.SemaphoreType.DMA((2, 2)),
                              pltpu.VMEM((SEG, PAGE), F32), pltpu.VMEM((SB_HEADS, PAGE), F32)],
        ),
        out_shape=[jax.ShapeDtypeStruct((nb, seq, SEG), F32),
                   jax.ShapeDtypeStruct((db, 1, SEG), F32)],
        compiler_params=pltpu.CompilerParams(
            dimension_semantics=("arbitrary", "arbitrary", "arbitrary"),
            vmem_limit_bytes=SB_VMEM_LIMIT),
        name="sb_attention",
    )(bias, page_table,
      sq, sk_all, sv_all, sgate,
      sq_s.reshape(db, 1, SEG), sgate_s.reshape(db, 1, SEG), bias_bc, ck, cv)

    n_chunks = seq // HG_CHUNK
    sums = _hgrn_sum_matrix(HG_CHUNK, True)
    lvl = _hgrn_level_index(HG_CHUNK)
    cspec = pl.BlockSpec((HG_BATCH, HG_CHUNK, SEG), lambda b, c: (b, c, 0))
    xspec = pl.BlockSpec((HG_BATCH, HG_CHUNK, d), lambda b, c: (b, c, 0))
    y_prompt, s_prompt = pl.pallas_call(
        _hgrn_prompt_kernel,
        grid=(nb // HG_BATCH, n_chunks),
        in_specs=[cspec, cspec, cspec, cspec,
                  pl.BlockSpec((1, SEG), lambda b, c: (0, 0)),
                  pl.BlockSpec((HG_HEADS, HG_DIM, HG_DIM), lambda b, c: (0, 0, 0)),
                  pl.BlockSpec(sums.shape, lambda b, c: (0, 0)),
                  pl.BlockSpec(lvl.shape, lambda b, c: (0, 0)),
                  xspec, cspec,
                  pl.BlockSpec(w_out_b.shape, lambda b, c: (0, 0))],
        out_specs=[xspec,
                   pl.BlockSpec((HG_BATCH, HG_HEADS, HG_DIM, HG_DIM), lambda b, c: (b, 0, 0, 0))],
        out_shape=[jax.ShapeDtypeStruct((nb, seq, d), F32),
                   jax.ShapeDtypeStruct((nb, HG_HEADS, HG_DIM, HG_DIM), F32)],
        scratch_shapes=[pltpu.VMEM((HG_BATCH, HG_HEADS, HG_DIM, HG_DIM), F32)],
        compiler_params=pltpu.CompilerParams(
            dimension_semantics=("parallel", "arbitrary"), vmem_limit_bytes=VMEM_LIMIT),
        name="hgrn_prompt",
    )(hq, lf, hv, hgate, ng, st_meta, sums, lvl, x_prompt, b_sb, w_out_b)

    y_sample = _merge(xs2d, a_hg_s, b_sb_s.reshape(db, SEG), w_out_b, db)

    return (y_prompt,
            y_sample.reshape(db, 1, d),
            sk_all.reshape(1, nb, N_META + seq, SB_HEADS, SB_DH),
            sv_all.reshape(1, nb, N_META + seq, SB_HEADS, SB_DH),
            s_prompt[None],
            sk_s.reshape(1, db, 1, SB_HEADS, SB_DH),
            sv_s.reshape(1, db, 1, SB_HEADS, SB_DH),
            s_sample[None])
```

```python
import functools

import jax
import jax.numpy as jnp
import numpy as np
from jax import lax
from jax.experimental import pallas as pl
from jax.experimental.pallas import tpu as pltpu

F32 = jnp.float32
BF16 = jnp.bfloat16
EPS = 1e-6

SEG = 512
N_SEG = 8
HG_HEADS = 4
HG_DIM = 128
SB_HEADS = 8
SB_DH = 64
PAGE = 128
MXU_TILE = 256
SUBLANES = 8
N_META = 16

HG_CHUNK = 128
HG_SMALL_LEVELS = (4, 2, 1)
HG_BATCH = 2
SB_BLK = 256
PROJ_ROWS = 512
DEC_PAGES = 32
STEP_SAMPLES = 8

VMEM_LIMIT = 56 * 1024 * 1024
SB_VMEM_LIMIT = 56 * 1024 * 1024

NT_DIMS = (((1,), (1,)), ((), ()))


def _dot(a, b):
    return jnp.dot(a, b, preferred_element_type=F32)


def _dot_nt(a, b):
    return lax.dot_general(a, b, NT_DIMS, preferred_element_type=F32)


def _split2(x):
    hi = x.astype(BF16)
    lo = (x - hi.astype(F32)).astype(BF16)
    return hi, lo


def _split3(x):
    hi = x.astype(BF16)
    r = x - hi.astype(F32)
    mid = r.astype(BF16)
    lo = (r - mid.astype(F32)).astype(BF16)
    return hi, mid, lo


def _softplus(z):
    return jnp.maximum(z, 0.0) + jnp.log(1.0 + jnp.exp(-jnp.abs(z)))


def _silu(z):
    return z * (1.0 / (1.0 + jnp.exp(-z)))


def _proj_kernel(x_ref, g_ref, w_ref, lbl_ref, qg_ref, kg_ref, grp_ref,
                 *rest, lead):
    lead_refs = rest[:-N_SEG]
    hq_ref, lf_ref, hv_ref, hgate_ref, sq_ref, sk_ref, sv_ref, sgate_ref = rest[-N_SEG:]
    ti = pl.program_id(1)
    tm = x_ref.shape[1]
    x = x_ref[0]
    ms = jnp.mean(x * x, axis=-1, keepdims=True)
    h = (x * lax.rsqrt(ms + EPS) * g_ref[...]).astype(BF16)

    def seg(j):
        return _dot(h, w_ref[:, j * SEG:(j + 1) * SEG])

    def group_rms(u, gain):
        hi, lo = _split2(u * u)
        gw = grp_ref.shape[0]
        ss = jnp.concatenate(
            [_dot(hi[:, c:c + gw], grp_ref[...]) + _dot(lo[:, c:c + gw], grp_ref[...])
             for c in range(0, SEG, gw)], axis=1)
        return u * lax.rsqrt(ss * (1.0 / SB_DH) + EPS) * gain

    hq_ref[0] = _silu(seg(0))

    l = lbl_ref[...]
    e = jnp.exp(l - jnp.max(l, axis=0, keepdims=True))
    lb = e[0:1, :] / jnp.sum(e, axis=0, keepdims=True)
    a = jnp.log(lb)
    z = seg(1)
    bb = jnp.log1p(-lb) + (jnp.minimum(z, 0.0) - jnp.log(1.0 + jnp.exp(-jnp.abs(z))))
    lf_ref[0] = jnp.maximum(a, bb) + jnp.log(1.0 + jnp.exp(-jnp.abs(a - bb)))

    hv_ref[0] = seg(2)
    hgate_ref[0] = _silu(seg(3))
    sq_ref[0] = group_rms(seg(4), qg_ref[...])
    sgate_ref[0] = _silu(seg(7))

    rows = pl.ds(pl.multiple_of(lead + ti * tm, SUBLANES), tm)
    sk_ref[0, rows, :] = group_rms(seg(5), kg_ref[...])
    sv_ref[0, rows, :] = seg(6)
    if lead:
        @pl.when(ti == 0)
        def _():
            sk_ref[0, 0:lead, :] = lead_refs[0][...]
            sv_ref[0, 0:lead, :] = lead_refs[1][...]


def _project(x3d, tm, norm_g, w_bf16, lb_logits, qg, kg, grp, lead_k=None, lead_v=None):
    n_seq, rows, d = x3d.shape
    lead = 0 if lead_k is None else lead_k.shape[0]
    lead_args = [] if lead_k is None else [lead_k, lead_v]
    const = lambda s, t: (0, 0)
    row_spec = pl.BlockSpec((1, tm, SEG), lambda s, t: (s, t, 0))
    row_shape = jax.ShapeDtypeStruct((n_seq, rows, SEG), F32)
    kv_spec = pl.BlockSpec((1, lead + rows, SEG), lambda s, t: (s, 0, 0))
    kv_shape = jax.ShapeDtypeStruct((n_seq, lead + rows, SEG), F32)
    return pl.pallas_call(
        functools.partial(_proj_kernel, lead=lead),
        grid=(n_seq, rows // tm),
        in_specs=[
            pl.BlockSpec((1, tm, d), lambda s, t: (s, t, 0)),
            pl.BlockSpec((1, d), const),
            pl.BlockSpec(w_bf16.shape, const),
            pl.BlockSpec(lb_logits.shape, const),
            pl.BlockSpec((1, SEG), const),
            pl.BlockSpec((1, SEG), const),
            pl.BlockSpec(grp.shape, const),
        ] + [pl.BlockSpec((lead, SEG), const)] * len(lead_args),
        out_specs=[row_spec] * 5 + [kv_spec, kv_spec, row_spec],
        out_shape=[row_shape] * 5 + [kv_shape, kv_shape, row_shape],
        compiler_params=pltpu.CompilerParams(
            dimension_semantics=("parallel", "arbitrary"), vmem_limit_bytes=VMEM_LIMIT),
        name="proj",
    )(x3d, norm_g, w_bf16, lb_logits, qg, kg, grp, *lead_args)


def _hgrn_sum_matrix(c, with_small_levels):
    t = np.arange(c)[:, None]
    j = np.arange(c)[None, :]
    blocks = [j <= t]
    if with_small_levels:
        for m in HG_SMALL_LEVELS[:-1]:
            f = (t // m) * m
            blocks.append((j > f) & (j <= t))
            blocks.append((j > t) & (j <= f + m))
        blocks.append(j == t + 1)
    return jnp.asarray(np.concatenate(blocks, axis=0), dtype=BF16)


def _hgrn_level_index(c):
    t = np.arange(c)[:, None]
    s = np.arange(c)[None, :]
    x = t ^ s
    lvl = np.where(x > 0, 2 ** np.floor(np.log2(np.maximum(x, 1))).astype(np.int64), 0)
    return jnp.asarray(np.where(s > t, -1, lvl), dtype=jnp.int32)


def _hgrn_chunk(q, lf, v, st, sums, lvl):
    c = q.shape[0]
    width = q.shape[1]
    want_out = lvl is not None
    summed = _dot(jnp.concatenate([sums, sums], axis=1), jnp.concatenate(_split2(lf), axis=0))

    def block(i):
        return summed[i * c:(i + 1) * c, :]

    b = block(0)
    kk = 1.0 - jnp.exp(lf)
    b_last = b[c - 1:c, :]
    kdec = (kk * jnp.exp(jnp.minimum(b_last - b, 0.0))).astype(BF16)
    sdec = jnp.exp(b_last)

    new_st = []
    for h in range(HG_HEADS):
        sl = slice(h * HG_DIM, (h + 1) * HG_DIM)
        vt = v[:, sl].T.astype(BF16)
        new_st.append(st[h] * sdec[:, sl] + _dot(vt, kdec[:, sl]))
    if not want_out:
        return new_st, None

    levels = [(0, q, kk)]
    m = c // 2
    while m > HG_SMALL_LEVELS[0]:
        nb = c // m
        starts = [b[i * m:i * m + 1, :] for i in range(nb)]
        hold = jnp.concatenate([jnp.broadcast_to(starts[i], (m, width)) for i in range(nb)], axis=0)
        nxt = jnp.concatenate(
            [jnp.broadcast_to(starts[min(i + 1, nb - 1)], (m, width)) for i in range(nb)], axis=0)
        levels.append((m, q * jnp.exp(b - hold), kk * jnp.exp(jnp.minimum(nxt - b, 0.0))))
        m //= 2
    for i, m in enumerate(HG_SMALL_LEVELS[:-1]):
        levels.append((m, q * jnp.exp(block(1 + 2 * i)), kk * jnp.exp(block(2 + 2 * i))))
    levels.append((1, q, kk * jnp.exp(block(2 * len(HG_SMALL_LEVELS) - 1))))

    qdec = (q * jnp.exp(b)).astype(BF16)
    levels = [(m, qm.astype(BF16), km) for m, qm, km in levels]

    assert c == HG_DIM
    zero = jnp.zeros((HG_DIM, HG_DIM), BF16)

    def blockdiag(y0, y1):
        return jnp.concatenate([jnp.concatenate([y0, zero], axis=1),
                                jnp.concatenate([zero, y1], axis=1)], axis=0)

    lvl2 = jnp.concatenate([lvl, lvl], axis=1)
    outs = []
    for h0 in range(0, HG_HEADS, 2):
        s0 = slice(h0 * HG_DIM, (h0 + 1) * HG_DIM)
        s1 = slice((h0 + 1) * HG_DIM, (h0 + 2) * HG_DIM)
        pair = slice(h0 * HG_DIM, (h0 + 2) * HG_DIM)
        a = jnp.zeros((c, 2 * c), F32)
        for m, qm, km in levels:
            kt = blockdiag(km[:, s0].T.astype(BF16), km[:, s1].T.astype(BF16))
            a = jnp.where(lvl2 == m, _dot(qm[:, pair], kt), a)
        o = _dot_nt(qdec[:, pair], blockdiag(st[h0].astype(BF16), st[h0 + 1].astype(BF16)))
        o = o + _dot(a.astype(BF16), blockdiag(v[:, s0].astype(BF16), v[:, s1].astype(BF16)))
        outs.extend([o[:, 0:HG_DIM], o[:, HG_DIM:2 * HG_DIM]])
    return new_st, outs


def _hgrn_meta_kernel(lf_ref, v_ref, sums_ref, st_ref):
    st0 = [jnp.zeros((HG_DIM, HG_DIM), F32)] * HG_HEADS
    lf = lf_ref[...]
    new_st, _ = _hgrn_chunk(lf, lf, v_ref[...], st0, sums_ref[...], None)
    for h in range(HG_HEADS):
        st_ref[h] = new_st[h]


def _hgrn_prompt_kernel(q_ref, lf_ref, v_ref, gate_ref, ng_ref, st0_ref, sums_ref, lvl_ref,
                        x_ref, sb_ref, w_ref, y_ref, s_ref, st_ref):
    ci = pl.program_id(1)
    n_batch = q_ref.shape[0]

    @pl.when(ci == 0)
    def _():
        for i in range(n_batch):
            st_ref[i] = st0_ref[...]

    final = []
    for i in range(n_batch):
        st = [st_ref[i, h] for h in range(HG_HEADS)]
        new_st, outs = _hgrn_chunk(q_ref[i], lf_ref[i], v_ref[i], st, sums_ref[...], lvl_ref[...])
        final.append(new_st)
        gated = []
        for h in range(HG_HEADS):
            sl = slice(h * HG_DIM, (h + 1) * HG_DIM)
            st_ref[i, h] = new_st[h]
            o = outs[h]
            ms = jnp.mean(o * o, axis=-1, keepdims=True)
            gated.append((o * lax.rsqrt(ms + EPS) * ng_ref[:, sl] * gate_ref[i, :, sl]).astype(BF16))
        a = jnp.concatenate(gated, axis=1)
        half = a.shape[1]
        y = _dot(a, w_ref[0:half, :]) + _dot(sb_ref[i].astype(BF16), w_ref[half:2 * half, :])
        y_ref[i] = x_ref[i] + y

    @pl.when(ci == pl.num_programs(1) - 1)
    def _():
        for i in range(n_batch):
            for h in range(HG_HEADS):
                s_ref[i, h] = final[i][h].T


def _hgrn_step_kernel(q_ref, lf_ref, v_ref, gate_ref, ng_ref, s0_ref, s_ref, a_ref):
    for i in range(STEP_SAMPLES):
        q = q_ref[i:i + 1, :]
        f = jnp.exp(lf_ref[i:i + 1, :])
        v = v_ref[i:i + 1, :]
        qk = q * (1.0 - f)
        for h in range(HG_HEADS):
            sl = slice(h * HG_DIM, (h + 1) * HG_DIM)
            fc = jnp.broadcast_to(f[:, sl], (HG_DIM, HG_DIM)).T
            sn = fc * s0_ref[i, h] + (1.0 - fc) * v[:, sl]
            s_ref[i, h] = sn
            qf = jnp.broadcast_to(q[:, sl] * f[:, sl], (8, HG_DIM)).astype(BF16)
            o = _dot(qf, s0_ref[i, h].astype(BF16))[0:1, :]
            o = o + jnp.sum(qk[:, sl], axis=-1, keepdims=True) * v[:, sl]
            ms = jnp.mean(o * o, axis=-1, keepdims=True)
            a_ref[i:i + 1, sl] = o * lax.rsqrt(ms + EPS) * ng_ref[:, sl] * gate_ref[i:i + 1, sl]


def _sb_weights(z2, biases, carries, mask, neg_from):
    tq = z2.shape[0] // 2
    new_carries, weights = [], []
    for h in range(2):
        z = z2[h * tq:(h + 1) * tq] + biases[h]
        sp = _softplus(z)
        if mask is not None:
            sp = jnp.where(mask, sp, 0.0)
        tail = _dot(sp.astype(BF16), neg_from)
        a = jnp.exp(z + tail + carries[h])
        if mask is not None:
            a = jnp.where(mask, a, 0.0)
        weights.append(a.astype(BF16))
        new_carries.append(carries[h] - jnp.sum(sp, axis=-1, keepdims=True))
    return tuple(new_carries), tuple(weights)


def _interleaved_block(order, n_blocks):
    half = order // 2
    return jnp.where(order % 2 == 0, half, n_blocks - 1 - half)


def _decode_chunk(qbd, bias, k_pages, v_pages, carry, acc, neg_from):
    n = len(k_pages)

    def lanes(x, i):
        return x[:, i * PAGE:(i + 1) * PAGE]

    k_all = jnp.concatenate([kp[...].astype(BF16) for kp in k_pages], axis=1)
    z = _dot(qbd, k_all) + bias
    sp = _softplus(z)
    sp_rows = jnp.concatenate([lanes(sp, i) for i in range(n)], axis=0)
    hi, lo = _split2(sp_rows)
    tail = _dot(hi, neg_from) + _dot(lo, neg_from)
    tot = jnp.sum(sp_rows, axis=-1, keepdims=True)
    weights = []
    for i in range(n):
        rows = slice(i * SB_HEADS, (i + 1) * SB_HEADS)
        weights.append(jnp.exp(lanes(z, i) + tail[rows] + carry))
        carry = carry - tot[rows]
    new_acc = []
    for h in range(SB_HEADS):
        rows = slice(h * SB_DH, (h + 1) * SB_DH)
        t = acc[h]
        for i in range(n):
            t = t + v_pages[i][rows, :] * jnp.broadcast_to(weights[i][h:h + 1, :], (SB_DH, PAGE))
        new_acc.append(t)
    return carry, new_acc


def _sb_kernel(bias_ref, pt_ref,
               q_ref, k_ref, v_ref, gate_ref,
               qd_ref, gated_ref, biasd_ref, ck_ref, cv_ref,
               o_ref, od_ref,
               kb_ref, v0_ref, v1_ref, kmb_ref, vm0_ref, vm1_ref,
               kbuf, vbuf, sem, acc_ref, carry_ref):
    p = pl.program_id(1)
    order = pl.program_id(2)
    qi = _interleaved_block(order, pl.num_programs(2))
    step = (pl.program_id(0) * pl.num_programs(1) + p) * pl.num_programs(2) + order
    n_steps = pl.num_programs(0) * pl.num_programs(1) * pl.num_programs(2)
    step_pages = kbuf.shape[1]
    n_pages = pt_ref.shape[1]
    steps_per_sample = n_pages // step_pages

    def page_copies(st, slot):
        sample = st // steps_per_sample
        first = (st % steps_per_sample) * step_pages
        copies = []
        for i in range(step_pages):
            page = pt_ref[sample, n_pages - 1 - (first + i)]
            copies.append(pltpu.make_async_copy(ck_ref.at[page], kbuf.at[slot, i], sem.at[0, slot]))
            copies.append(pltpu.make_async_copy(cv_ref.at[page], vbuf.at[slot, i], sem.at[1, slot]))
        return copies

    slot = step % 2

    @pl.when(step == 0)
    def _():
        for c in page_copies(step, slot):
            c.start()

    lane = lax.broadcasted_iota(jnp.int32, (1, 2 * SB_DH), 1)
    head0 = lane < SB_DH

    @pl.when(order == 0)
    def _():
        kb_ref[...] = k_ref[0, N_META:, :].astype(BF16)
        v = v_ref[0, N_META:, :]
        v0_ref[...] = jnp.where(head0, v, 0.0).astype(BF16)
        v1_ref[...] = jnp.where(head0, 0.0, v).astype(BF16)
        for ref in (kmb_ref, vm0_ref, vm1_ref):
            ref[...] = jnp.zeros_like(ref)
        kmb_ref[0:N_META, :] = k_ref[0, 0:N_META, :].astype(BF16)
        vm = v_ref[0, 0:N_META, :]
        vm0_ref[0:N_META, :] = jnp.where(head0, vm, 0.0).astype(BF16)
        vm1_ref[0:N_META, :] = jnp.where(head0, 0.0, vm).astype(BF16)

    q = q_ref[0] * (SB_DH ** -0.5)
    q2 = jnp.concatenate([jnp.where(head0, q, 0.0), jnp.where(head0, 0.0, q)], axis=0).astype(BF16)
    biases = (bias_ref[2 * p], bias_ref[2 * p + 1])
    r = lax.broadcasted_iota(jnp.int32, (SB_BLK, SB_BLK), 0)
    s = lax.broadcasted_iota(jnp.int32, (SB_BLK, SB_BLK), 1)
    neg_from = jnp.where(r >= s, -1.0, 0.0).astype(BF16)
    diag_mask = s < r
    rm = lax.broadcasted_iota(jnp.int32, (PAGE, PAGE), 0)
    sm = lax.broadcasted_iota(jnp.int32, (PAGE, PAGE), 1)
    neg_from_m = jnp.where(rm >= sm, -1.0, 0.0).astype(BF16)
    meta_mask = lax.broadcasted_iota(jnp.int32, (SB_BLK, PAGE), 1) < N_META

    def scores(blk):
        return _dot_nt(q2, kb_ref[pl.ds(pl.multiple_of(blk * SB_BLK, SB_BLK), SB_BLK), :])

    def weighted_values(acc, weights, blk):
        rows = pl.ds(pl.multiple_of(blk * SB_BLK, SB_BLK), SB_BLK)
        return acc + _dot(weights[0], v0_ref[rows, :]) + _dot(weights[1], v1_ref[rows, :])

    ahead = jnp.where(step + 1 < n_steps, step + 1, 0)
    for c in page_copies(ahead, 1 - slot):
        c.start()

    zero_c = jnp.zeros((SB_BLK, 1), F32)
    z_meta = _dot_nt(q2, kmb_ref[...])
    carries, weights = _sb_weights(scores(qi), biases, (zero_c, zero_c), diag_mask, neg_from)
    z_next = scores(jnp.maximum(qi - 1, 0))

    def body(n, state):
        carries, acc, z_cur, w_prev = state
        acc = weighted_values(acc, w_prev, qi - n)
        z_nxt = scores(jnp.maximum(qi - 2 - n, 0))
        carries, w_cur = _sb_weights(z_cur, biases, carries, None, neg_from)
        return carries, acc, z_nxt, w_cur

    carries, acc, _, weights = lax.fori_loop(
        0, qi, body, (carries, jnp.zeros((SB_BLK, 2 * SB_DH), F32), z_next, weights))
    acc = weighted_values(acc, weights, 0)

    for c in page_copies(step, slot):
        c.wait()
    part = step % steps_per_sample

    @pl.when(part == 0)
    def _():
        acc_ref[...] = jnp.zeros_like(acc_ref)
        carry_ref[...] = jnp.zeros_like(carry_ref)

    width = SB_HEADS * SB_DH
    hrow = lax.broadcasted_iota(jnp.int32, (SB_HEADS, width), 0)
    hlane = lax.broadcasted_iota(jnp.int32, (SB_HEADS, width), 1) // SB_DH
    own = hrow == hlane
    qd = jnp.broadcast_to(qd_ref[0] * (SB_DH ** -0.5), (SB_HEADS, width))
    qbd = jnp.where(own, qd, 0.0).astype(BF16)
    carry_d = carry_ref[...]
    acc_d = [acc_ref[h * SB_DH:(h + 1) * SB_DH, :] for h in range(SB_HEADS)]
    for c0 in range(0, step_pages, DEC_PAGES):
        idx = range(c0, c0 + DEC_PAGES)
        carry_d, acc_d = _decode_chunk(qbd, biasd_ref[...], [kbuf.at[slot, i] for i in idx],
                                       [vbuf.at[slot, i] for i in idx], carry_d, acc_d, neg_from_m)
    for h in range(SB_HEADS):
        acc_ref[h * SB_DH:(h + 1) * SB_DH, :] = acc_d[h]
    carry_ref[...] = carry_d

    @pl.when(part == steps_per_sample - 1)
    def _():
        od = jnp.sum(acc_ref[...].T, axis=0, keepdims=True)
        od_ref[0] = od * gated_ref[0]

    _, w_meta = _sb_weights(z_meta, biases, carries, meta_mask, neg_from_m)
    acc = acc + _dot(w_meta[0], vm0_ref[...]) + _dot(w_meta[1], vm1_ref[...])
    o_ref[0] = acc * gate_ref[0]

    @pl.when(step == n_steps - 1)
    def _():
        for c in page_copies(ahead, 1 - slot):
            c.wait()


def _merge_kernel(x_ref, a_ref, b_ref, w_ref, y_ref):
    half = a_ref.shape[-1]
    y = _dot(a_ref[...].astype(BF16), w_ref[0:half, :])
    y = y + _dot(b_ref[...].astype(BF16), w_ref[half:2 * half, :])
    y_ref[...] = x_ref[...] + y


def _merge(x2d, a, b, w_bf16, tm):
    rows, d = x2d.shape
    half = a.shape[-1]
    return pl.pallas_call(
        _merge_kernel,
        grid=(rows // tm,),
        in_specs=[
            pl.BlockSpec((tm, d), lambda i: (i, 0)),
            pl.BlockSpec((tm, half), lambda i: (i, 0)),
            pl.BlockSpec((tm, half), lambda i: (i, 0)),
            pl.BlockSpec(w_bf16.shape, lambda i: (0, 0)),
        ],
        out_specs=pl.BlockSpec((tm, d), lambda i: (i, 0)),
        out_shape=jax.ShapeDtypeStruct((rows, d), F32),
        compiler_params=pltpu.CompilerParams(
            dimension_semantics=("parallel",), vmem_limit_bytes=VMEM_LIMIT),
        name="merge",
    )(x2d, a, b, w_bf16)


def kernel(x_prompt, x_sample, cache_k, cache_v, state_hgrn, page_table, meta_tokens, norm_g,
           w_in, lb_logits, hg_norm_g, q_norm_g, k_norm_g, sb_bias, w_out):
    nb, seq, d = x_prompt.shape
    db = x_sample.shape[0]
    n_pages = page_table.shape[1]
    n_phys = cache_k.shape[1]
    assert w_in.shape[0] == 1 and w_in.shape[2] == N_SEG * SEG
    assert seq % SB_BLK == 0 and seq % HG_CHUNK == 0 and nb % HG_BATCH == 0
    assert db % STEP_SAMPLES == 0 and seq % PROJ_ROWS == 0

    w_in_b = w_in[0].astype(BF16)
    w_out_b = w_out[0].astype(BF16)
    g_row = norm_g[0][None, :]
    qg = jnp.tile(q_norm_g[0], SB_HEADS)[None, :]
    kg = jnp.tile(k_norm_g[0], SB_HEADS)[None, :]
    ng = hg_norm_g[0].reshape(1, HG_HEADS * HG_DIM)
    lane_head = jnp.arange(MXU_TILE) // SB_DH
    grp = (lane_head[:, None] == lane_head[None, :]).astype(BF16)
    bias = sb_bias[0].astype(F32)

    xs2d = x_sample.reshape(db, d)
    small = jnp.concatenate([xs2d, meta_tokens.astype(F32)], axis=0)
    n_small = small.shape[0]
    s_out = _project(small[None], n_small, g_row, w_in_b, lb_logits, qg, kg, grp)
    hq_s, lf_s, hv_s, hgate_s, sq_s, sk_s, sv_s, sgate_s = [a[0, :db] for a in s_out]
    _, lf_m, hv_m, _, _, sk_m, sv_m, _ = [a[0, db:] for a in s_out]
    hq, lf, hv, hgate, sq, sk_all, sv_all, sgate = _project(
        x_prompt, PROJ_ROWS, g_row, w_in_b, lb_logits, qg, kg, grp, lead_k=sk_m, lead_v=sv_m)

    pad_m = ((0, HG_CHUNK - N_META), (0, 0))
    st_meta = pl.pallas_call(
        _hgrn_meta_kernel,
        out_shape=jax.ShapeDtypeStruct((HG_HEADS, HG_DIM, HG_DIM), F32),
        name="hgrn_meta",
    )(jnp.pad(lf_m, pad_m), jnp.pad(hv_m, pad_m), _hgrn_sum_matrix(HG_CHUNK, False))

    rspec = pl.BlockSpec((STEP_SAMPLES, SEG), lambda i: (i, 0))
    sspec = pl.BlockSpec((STEP_SAMPLES, HG_HEADS, HG_DIM, HG_DIM), lambda i: (i, 0, 0, 0))
    s_sample, a_hg_s = pl.pallas_call(
        _hgrn_step_kernel,
        grid=(db // STEP_SAMPLES,),
        in_specs=[rspec, rspec, rspec, rspec, pl.BlockSpec((1, SEG), lambda i: (0, 0)), sspec],
        out_specs=[sspec, rspec],
        out_shape=[jax.ShapeDtypeStruct((db, HG_HEADS, HG_DIM, HG_DIM), F32),
                   jax.ShapeDtypeStruct((db, SEG), F32)],
        compiler_params=pltpu.CompilerParams(
            dimension_semantics=("parallel",), vmem_limit_bytes=VMEM_LIMIT),
        name="hgrn_step",
    )(hq_s, lf_s, hv_s, hgate_s, ng, state_hgrn[0])

    n_qb = seq // SB_BLK
    n_pairs = SB_HEADS // 2
    n_steps = nb * n_pairs * n_qb
    assert (db * n_pages) % n_steps == 0
    step_pages = db * n_pages // n_steps
    assert n_pages % step_pages == 0 and step_pages % DEC_PAGES == 0
    steps_per_sample = n_pages // step_pages
    ck = jnp.transpose(cache_k[0], (0, 2, 3, 1)).reshape(n_phys, SEG, PAGE)
    cv = jnp.transpose(cache_v[0], (0, 2, 3, 1)).reshape(n_phys, SEG, PAGE)
    bias_bc = jnp.broadcast_to(bias[:, None], (SB_HEADS, DEC_PAGES * PAGE))

    qspec = pl.BlockSpec((1, SB_BLK, 2 * SB_DH),
                         lambda b, p, i, *_: (b, _interleaved_block(i, n_qb), p))
    kvspec = pl.BlockSpec((1, N_META + seq, 2 * SB_DH), lambda b, p, i, *_: (b, 0, p))
    vec_spec = pl.BlockSpec(
        (1, 1, SEG), lambda b, p, i, *_: (((b * n_pairs + p) * n_qb + i) // steps_per_sample, 0, 0))
    hbm_spec = pl.BlockSpec(memory_space=pl.ANY)
    b_sb, b_sb_s = pl.pallas_call(
        _sb_kernel,
        grid_spec=pltpu.PrefetchScalarGridSpec(
            num_scalar_prefetch=2,
            grid=(nb, n_pairs, n_qb),
            in_specs=[qspec, kvspec, kvspec, qspec,
                      vec_spec, vec_spec,
                      pl.BlockSpec((SB_HEADS, DEC_PAGES * PAGE), lambda b, p, i, *_: (0, 0)),
                      hbm_spec, hbm_spec],
            out_specs=[qspec, vec_spec],
            scratch_shapes=[pltpu.VMEM((seq, 2 * SB_DH), BF16)] * 3
                           + [pltpu.VMEM((PAGE, 2 * SB_DH), BF16)] * 3
                           + [pltpu.VMEM((2, step_pages, SEG, PAGE), F32)] * 2
                           + [pltpu.SemaphoreType.DMA((2, 2)),
                              pltpu.VMEM((SEG, PAGE), F32), pltpu.VMEM((SB_HEADS, PAGE), F32)],
        ),
        out_shape=[jax.ShapeDtypeStruct((nb, seq, SEG), F32),
                   jax.ShapeDtypeStruct((db, 1, SEG), F32)],
        compiler_params=pltpu.CompilerParams(
            dimension_semantics=("arbitrary", "arbitrary", "arbitrary"),
            vmem_limit_bytes=SB_VMEM_LIMIT),
        name="sb_attention",
    )(bias, page_table,
      sq, sk_all, sv_all, sgate,
      sq_s.reshape(db, 1, SEG), sgate_s.reshape(db, 1, SEG), bias_bc, ck, cv)

    n_chunks = seq // HG_CHUNK
    sums = _hgrn_sum_matrix(HG_CHUNK, True)
    lvl = _hgrn_level_index(HG_CHUNK)
    cspec = pl.BlockSpec((HG_BATCH, HG_CHUNK, SEG), lambda b, c: (b, c, 0))
    xspec = pl.BlockSpec((HG_BATCH, HG_CHUNK, d), lambda b, c: (b, c, 0))
    y_prompt, s_prompt = pl.pallas_call(
        _hgrn_prompt_kernel,
        grid=(nb // HG_BATCH, n_chunks),
        in_specs=[cspec, cspec, cspec, cspec,
                  pl.BlockSpec((1, SEG), lambda b, c: (0, 0)),
                  pl.BlockSpec((HG_HEADS, HG_DIM, HG_DIM), lambda b, c: (0, 0, 0)),
                  pl.BlockSpec(sums.shape, lambda b, c: (0, 0)),
                  pl.BlockSpec(lvl.shape, lambda b, c: (0, 0)),
                  xspec, cspec,
                  pl.BlockSpec(w_out_b.shape, lambda b, c: (0, 0))],
        out_specs=[xspec,
                   pl.BlockSpec((HG_BATCH, HG_HEADS, HG_DIM, HG_DIM), lambda b, c: (b, 0, 0, 0))],
        out_shape=[jax.ShapeDtypeStruct((nb, seq, d), F32),
                   jax.ShapeDtypeStruct((nb, HG_HEADS, HG_DIM, HG_DIM), F32)],
        scratch_shapes=[pltpu.VMEM((HG_BATCH, HG_HEADS, HG_DIM, HG_DIM), F32)],
        compiler_params=pltpu.CompilerParams(
            dimension_semantics=("parallel", "arbitrary"), vmem_limit_bytes=VMEM_LIMIT),
        name="hgrn_prompt",
    )(hq, lf, hv, hgate, ng, st_meta, sums, lvl, x_prompt, b_sb, w_out_b)

    y_sample = _merge(xs2d, a_hg_s, b_sb_s.reshape(db, SEG), w_out_b, db)

    return (y_prompt,
            y_sample.reshape(db, 1, d),
            sk_all.reshape(1, nb, N_META + seq, SB_HEADS, SB_DH),
            sv_all.reshape(1, nb, N_META + seq, SB_HEADS, SB_DH),
            s_prompt[None],
            sk_s.reshape(1, db, 1, SB_HEADS, SB_DH),
            sv_s.reshape(1, db, 1, SB_HEADS, SB_DH),
            s_sample[None])
```

```python
import functools

import jax
import jax.numpy as jnp
import numpy as np
from jax import lax
from jax.experimental import pallas as pl
from jax.experimental.pallas import tpu as pltpu

F32 = jnp.float32
BF16 = jnp.bfloat16
EPS = 1e-6
LOG2E = 1.4426950408889634

SEG = 512
N_SEG = 8
HG_HEADS = 4
HG_DIM = 128
SB_HEADS = 8
SB_DH = 64
PAGE = 128
MXU_TILE = 256
SUBLANES = 8
N_META = 16

HG_CHUNK = 128
HG_SMALL_LEVELS = (4, 2, 1)
HG_BATCH = 2
SB_BLK = 256
PROJ_ROWS = 512
DEC_PAGES = 32
STEP_SAMPLES = 8

VMEM_LIMIT = 56 * 1024 * 1024
SB_VMEM_LIMIT = 56 * 1024 * 1024

NT_DIMS = (((1,), (1,)), ((), ()))


def _dot(a, b):
    return jnp.dot(a, b, preferred_element_type=F32)


def _dot_nt(a, b):
    return lax.dot_general(a, b, NT_DIMS, preferred_element_type=F32)


def _split2(x):
    hi = x.astype(BF16)
    lo = (x - hi.astype(F32)).astype(BF16)
    return hi, lo


def _split3(x):
    hi = x.astype(BF16)
    r = x - hi.astype(F32)
    mid = r.astype(BF16)
    lo = (r - mid.astype(F32)).astype(BF16)
    return hi, mid, lo


def _softplus(z):
    return jnp.maximum(z, 0.0) + jnp.log(1.0 + jnp.exp(-jnp.abs(z)))


def _silu(z):
    return z * (1.0 / (1.0 + jnp.exp(-z)))


def _proj_kernel(x_ref, g_ref, w_ref, lbl_ref, qg_ref, kg_ref, grp_ref,
                 *rest, lead):
    lead_refs = rest[:-N_SEG]
    hq_ref, lf_ref, hv_ref, hgate_ref, sq_ref, sk_ref, sv_ref, sgate_ref = rest[-N_SEG:]
    ti = pl.program_id(1)
    tm = x_ref.shape[1]
    x = x_ref[0]
    ms = jnp.mean(x * x, axis=-1, keepdims=True)
    h = (x * lax.rsqrt(ms + EPS) * g_ref[...]).astype(BF16)

    def seg(j):
        return _dot(h, w_ref[:, j * SEG:(j + 1) * SEG])

    def group_rms(u, gain):
        hi, lo = _split2(u * u)
        gw = grp_ref.shape[0]
        ss = jnp.concatenate(
            [_dot(hi[:, c:c + gw], grp_ref[...]) + _dot(lo[:, c:c + gw], grp_ref[...])
             for c in range(0, SEG, gw)], axis=1)
        return u * lax.rsqrt(ss * (1.0 / SB_DH) + EPS) * gain

    hq_ref[0] = _silu(seg(0))

    l = lbl_ref[...]
    e = jnp.exp(l - jnp.max(l, axis=0, keepdims=True))
    lb = e[0:1, :] / jnp.sum(e, axis=0, keepdims=True)
    a = jnp.log(lb)
    z = seg(1)
    bb = jnp.log1p(-lb) + (jnp.minimum(z, 0.0) - jnp.log(1.0 + jnp.exp(-jnp.abs(z))))
    lf_ref[0] = jnp.maximum(a, bb) + jnp.log(1.0 + jnp.exp(-jnp.abs(a - bb)))

    hv_ref[0] = seg(2)
    hgate_ref[0] = _silu(seg(3))
    sq_ref[0] = group_rms(seg(4), qg_ref[...])
    sgate_ref[0] = _silu(seg(7))

    rows = pl.ds(pl.multiple_of(lead + ti * tm, SUBLANES), tm)
    sk_ref[0, rows, :] = group_rms(seg(5), kg_ref[...])
    sv_ref[0, rows, :] = seg(6)
    if lead:
        @pl.when(ti == 0)
        def _():
            sk_ref[0, 0:lead, :] = lead_refs[0][...]
            sv_ref[0, 0:lead, :] = lead_refs[1][...]


def _project(x3d, tm, norm_g, w_bf16, lb_logits, qg, kg, grp, lead_k=None, lead_v=None):
    n_seq, rows, d = x3d.shape
    lead = 0 if lead_k is None else lead_k.shape[0]
    lead_args = [] if lead_k is None else [lead_k, lead_v]
    const = lambda s, t: (0, 0)
    row_spec = pl.BlockSpec((1, tm, SEG), lambda s, t: (s, t, 0))
    row_shape = jax.ShapeDtypeStruct((n_seq, rows, SEG), F32)
    kv_spec = pl.BlockSpec((1, lead + rows, SEG), lambda s, t: (s, 0, 0))
    kv_shape = jax.ShapeDtypeStruct((n_seq, lead + rows, SEG), F32)
    return pl.pallas_call(
        functools.partial(_proj_kernel, lead=lead),
        grid=(n_seq, rows // tm),
        in_specs=[
            pl.BlockSpec((1, tm, d), lambda s, t: (s, t, 0)),
            pl.BlockSpec((1, d), const),
            pl.BlockSpec(w_bf16.shape, const),
            pl.BlockSpec(lb_logits.shape, const),
            pl.BlockSpec((1, SEG), const),
            pl.BlockSpec((1, SEG), const),
            pl.BlockSpec(grp.shape, const),
        ] + [pl.BlockSpec((lead, SEG), const)] * len(lead_args),
        out_specs=[row_spec] * 5 + [kv_spec, kv_spec, row_spec],
        out_shape=[row_shape] * 5 + [kv_shape, kv_shape, row_shape],
        compiler_params=pltpu.CompilerParams(
            dimension_semantics=("parallel", "arbitrary"), vmem_limit_bytes=VMEM_LIMIT),
        name="proj",
    )(x3d, norm_g, w_bf16, lb_logits, qg, kg, grp, *lead_args)


def _hgrn_sum_matrix(c, with_small_levels):
    t = np.arange(c)[:, None]
    j = np.arange(c)[None, :]
    blocks = [j <= t]
    if with_small_levels:
        for m in HG_SMALL_LEVELS[:-1]:
            f = (t // m) * m
            blocks.append((j > f) & (j <= t))
            blocks.append((j > t) & (j <= f + m))
        blocks.append(j == t + 1)
    return jnp.asarray(np.concatenate(blocks, axis=0), dtype=BF16)


def _hgrn_level_index(c):
    t = np.arange(c)[:, None]
    s = np.arange(c)[None, :]
    x = t ^ s
    lvl = np.where(x > 0, 2 ** np.floor(np.log2(np.maximum(x, 1))).astype(np.int64), 0)
    return jnp.asarray(np.where(s > t, -1, lvl), dtype=jnp.int32)


def _hgrn_chunk(q, lf, v, st, sums, lvl):
    c = q.shape[0]
    width = q.shape[1]
    want_out = lvl is not None
    lf = lf * LOG2E
    summed = _dot(jnp.concatenate([sums, sums], axis=1), jnp.concatenate(_split2(lf), axis=0))

    def block(i):
        return summed[i * c:(i + 1) * c, :]

    b = block(0)
    kk = 1.0 - jnp.exp2(lf)
    b_last = b[c - 1:c, :]
    kdec = (kk * jnp.exp2(jnp.minimum(b_last - b, 0.0))).astype(BF16)
    sdec = jnp.exp2(b_last)

    new_st = []
    for h in range(HG_HEADS):
        sl = slice(h * HG_DIM, (h + 1) * HG_DIM)
        vt = v[:, sl].T.astype(BF16)
        new_st.append(st[h] * sdec[:, sl] + _dot(vt, kdec[:, sl]))
    if not want_out:
        return new_st, None

    levels = [(0, q, kk)]
    m = c // 2
    while m > HG_SMALL_LEVELS[0]:
        nb = c // m
        starts = [b[i * m:i * m + 1, :] for i in range(nb)]
        hold = jnp.concatenate([jnp.broadcast_to(starts[i], (m, width)) for i in range(nb)], axis=0)
        nxt = jnp.concatenate(
            [jnp.broadcast_to(starts[min(i + 1, nb - 1)], (m, width)) for i in range(nb)], axis=0)
        levels.append((m, q * jnp.exp2(b - hold), kk * jnp.exp2(jnp.minimum(nxt - b, 0.0))))
        m //= 2
    for i, m in enumerate(HG_SMALL_LEVELS[:-1]):
        levels.append((m, q * jnp.exp2(block(1 + 2 * i)), kk * jnp.exp2(block(2 + 2 * i))))
    levels.append((1, q, kk * jnp.exp2(block(2 * len(HG_SMALL_LEVELS) - 1))))

    qdec = (q * jnp.exp2(b)).astype(BF16)
    levels = [(m, qm.astype(BF16), km) for m, qm, km in levels]

    assert c == HG_DIM
    zero = jnp.zeros((HG_DIM, HG_DIM), BF16)

    def blockdiag(y0, y1):
        return jnp.concatenate([jnp.concatenate([y0, zero], axis=1),
                                jnp.concatenate([zero, y1], axis=1)], axis=0)

    lvl2 = jnp.concatenate([lvl, lvl], axis=1)
    outs = []
    for h0 in range(0, HG_HEADS, 2):
        s0 = slice(h0 * HG_DIM, (h0 + 1) * HG_DIM)
        s1 = slice((h0 + 1) * HG_DIM, (h0 + 2) * HG_DIM)
        pair = slice(h0 * HG_DIM, (h0 + 2) * HG_DIM)
        a = jnp.zeros((c, 2 * c), F32)
        for m, qm, km in levels:
            kt = blockdiag(km[:, s0].T.astype(BF16), km[:, s1].T.astype(BF16))
            a = jnp.where(lvl2 == m, _dot(qm[:, pair], kt), a)
        o = _dot_nt(qdec[:, pair], blockdiag(st[h0].astype(BF16), st[h0 + 1].astype(BF16)))
        o = o + _dot(a.astype(BF16), blockdiag(v[:, s0].astype(BF16), v[:, s1].astype(BF16)))
        outs.extend([o[:, 0:HG_DIM], o[:, HG_DIM:2 * HG_DIM]])
    return new_st, outs


def _hgrn_meta_kernel(lf_ref, v_ref, sums_ref, st_ref):
    st0 = [jnp.zeros((HG_DIM, HG_DIM), F32)] * HG_HEADS
    lf = lf_ref[...]
    new_st, _ = _hgrn_chunk(lf, lf, v_ref[...], st0, sums_ref[...], None)
    for h in range(HG_HEADS):
        st_ref[h] = new_st[h]


def _hgrn_prompt_kernel(q_ref, lf_ref, v_ref, gate_ref, ng_ref, st0_ref, sums_ref, lvl_ref,
                        x_ref, sb_ref, w_ref, y_ref, s_ref, st_ref):
    ci = pl.program_id(1)
    n_batch = q_ref.shape[0]

    @pl.when(ci == 0)
    def _():
        for i in range(n_batch):
            st_ref[i] = st0_ref[...]

    final = []
    for i in range(n_batch):
        st = [st_ref[i, h] for h in range(HG_HEADS)]
        new_st, outs = _hgrn_chunk(q_ref[i], lf_ref[i], v_ref[i], st, sums_ref[...], lvl_ref[...])
        final.append(new_st)
        gated = []
        for h in range(HG_HEADS):
            sl = slice(h * HG_DIM, (h + 1) * HG_DIM)
            st_ref[i, h] = new_st[h]
            o = outs[h]
            ms = jnp.mean(o * o, axis=-1, keepdims=True)
            gated.append((o * lax.rsqrt(ms + EPS) * ng_ref[:, sl] * gate_ref[i, :, sl]).astype(BF16))
        a = jnp.concatenate(gated, axis=1)
        half = a.shape[1]
        y = _dot(a, w_ref[0:half, :]) + _dot(sb_ref[i].astype(BF16), w_ref[half:2 * half, :])
        y_ref[i] = x_ref[i] + y

    @pl.when(ci == pl.num_programs(1) - 1)
    def _():
        for i in range(n_batch):
            for h in range(HG_HEADS):
                s_ref[i, h] = final[i][h].T


def _hgrn_step_kernel(q_ref, lf_ref, v_ref, gate_ref, ng_ref, s0_ref, s_ref, a_ref):
    for i in range(STEP_SAMPLES):
        q = q_ref[i:i + 1, :]
        f = jnp.exp(lf_ref[i:i + 1, :])
        v = v_ref[i:i + 1, :]
        qk = q * (1.0 - f)
        for h in range(HG_HEADS):
            sl = slice(h * HG_DIM, (h + 1) * HG_DIM)
            fc = jnp.broadcast_to(f[:, sl], (HG_DIM, HG_DIM)).T
            sn = fc * s0_ref[i, h] + (1.0 - fc) * v[:, sl]
            s_ref[i, h] = sn
            qf = jnp.broadcast_to(q[:, sl] * f[:, sl], (8, HG_DIM)).astype(BF16)
            o = _dot(qf, s0_ref[i, h].astype(BF16))[0:1, :]
            o = o + jnp.sum(qk[:, sl], axis=-1, keepdims=True) * v[:, sl]
            ms = jnp.mean(o * o, axis=-1, keepdims=True)
            a_ref[i:i + 1, sl] = o * lax.rsqrt(ms + EPS) * ng_ref[:, sl] * gate_ref[i:i + 1, sl]


def _sb_weights(z2, biases, carries, mask, neg_from):
    tq = z2.shape[0] // 2
    new_carries, weights = [], []
    for h in range(2):
        z = z2[h * tq:(h + 1) * tq] + biases[h]
        sp = _softplus(z)
        if mask is not None:
            sp = jnp.where(mask, sp, 0.0)
        tail = _dot(sp.astype(BF16), neg_from)
        a = jnp.exp(z + tail + carries[h])
        if mask is not None:
            a = jnp.where(mask, a, 0.0)
        weights.append(a.astype(BF16))
        new_carries.append(carries[h] - jnp.sum(sp, axis=-1, keepdims=True))
    return tuple(new_carries), tuple(weights)


def _interleaved_block(order, n_blocks):
    half = order // 2
    return jnp.where(order % 2 == 0, half, n_blocks - 1 - half)


def _decode_chunk(qbd, bias, k_pages, v_pages, carry, acc, neg_from):
    n = len(k_pages)

    def lanes(x, i):
        return x[:, i * PAGE:(i + 1) * PAGE]

    k_all = jnp.concatenate([kp[...].astype(BF16) for kp in k_pages], axis=1)
    z = _dot(qbd, k_all) + bias
    sp = _softplus(z)
    sp_rows = jnp.concatenate([lanes(sp, i) for i in range(n)], axis=0)
    hi, lo = _split2(sp_rows)
    tail = _dot(hi, neg_from) + _dot(lo, neg_from)
    tot = jnp.sum(sp_rows, axis=-1, keepdims=True)
    weights = []
    for i in range(n):
        rows = slice(i * SB_HEADS, (i + 1) * SB_HEADS)
        weights.append(jnp.exp(lanes(z, i) + tail[rows] + carry))
        carry = carry - tot[rows]
    new_acc = []
    for h in range(SB_HEADS):
        rows = slice(h * SB_DH, (h + 1) * SB_DH)
        t = acc[h]
        for i in range(n):
            t = t + v_pages[i][rows, :] * jnp.broadcast_to(weights[i][h:h + 1, :], (SB_DH, PAGE))
        new_acc.append(t)
    return carry, new_acc


def _sb_kernel(bias_ref, pt_ref,
               q_ref, k_ref, v_ref, gate_ref,
               qd_ref, gated_ref, biasd_ref, ck_ref, cv_ref,
               o_ref, od_ref,
               kb_ref, v0_ref, v1_ref, kmb_ref, vm0_ref, vm1_ref,
               kbuf, vbuf, sem, acc_ref, carry_ref):
    p = pl.program_id(1)
    order = pl.program_id(2)
    qi = _interleaved_block(order, pl.num_programs(2))
    step = (pl.program_id(0) * pl.num_programs(1) + p) * pl.num_programs(2) + order
    n_steps = pl.num_programs(0) * pl.num_programs(1) * pl.num_programs(2)
    step_pages = kbuf.shape[1]
    n_pages = pt_ref.shape[1]
    steps_per_sample = n_pages // step_pages

    def page_copies(st, slot):
        sample = st // steps_per_sample
        first = (st % steps_per_sample) * step_pages
        copies = []
        for i in range(step_pages):
            page = pt_ref[sample, n_pages - 1 - (first + i)]
            copies.append(pltpu.make_async_copy(ck_ref.at[page], kbuf.at[slot, i], sem.at[0, slot]))
            copies.append(pltpu.make_async_copy(cv_ref.at[page], vbuf.at[slot, i], sem.at[1, slot]))
        return copies

    slot = step % 2

    @pl.when(step == 0)
    def _():
        for c in page_copies(step, slot):
            c.start()

    lane = lax.broadcasted_iota(jnp.int32, (1, 2 * SB_DH), 1)
    head0 = lane < SB_DH

    @pl.when(order == 0)
    def _():
        kb_ref[...] = k_ref[0, N_META:, :].astype(BF16)
        v = v_ref[0, N_META:, :]
        v0_ref[...] = jnp.where(head0, v, 0.0).astype(BF16)
        v1_ref[...] = jnp.where(head0, 0.0, v).astype(BF16)
        for ref in (kmb_ref, vm0_ref, vm1_ref):
            ref[...] = jnp.zeros_like(ref)
        kmb_ref[0:N_META, :] = k_ref[0, 0:N_META, :].astype(BF16)
        vm = v_ref[0, 0:N_META, :]
        vm0_ref[0:N_META, :] = jnp.where(head0, vm, 0.0).astype(BF16)
        vm1_ref[0:N_META, :] = jnp.where(head0, 0.0, vm).astype(BF16)

    q = q_ref[0] * (SB_DH ** -0.5)
    q2 = jnp.concatenate([jnp.where(head0, q, 0.0), jnp.where(head0, 0.0, q)], axis=0).astype(BF16)
    biases = (bias_ref[2 * p], bias_ref[2 * p + 1])
    r = lax.broadcasted_iota(jnp.int32, (SB_BLK, SB_BLK), 0)
    s = lax.broadcasted_iota(jnp.int32, (SB_BLK, SB_BLK), 1)
    neg_from = jnp.where(r >= s, -1.0, 0.0).astype(BF16)
    diag_mask = s < r
    rm = lax.broadcasted_iota(jnp.int32, (PAGE, PAGE), 0)
    sm = lax.broadcasted_iota(jnp.int32, (PAGE, PAGE), 1)
    neg_from_m = jnp.where(rm >= sm, -1.0, 0.0).astype(BF16)
    meta_mask = lax.broadcasted_iota(jnp.int32, (SB_BLK, PAGE), 1) < N_META

    def scores(blk):
        return _dot_nt(q2, kb_ref[pl.ds(pl.multiple_of(blk * SB_BLK, SB_BLK), SB_BLK), :])

    def weighted_values(acc, weights, blk):
        rows = pl.ds(pl.multiple_of(blk * SB_BLK, SB_BLK), SB_BLK)
        return acc + _dot(weights[0], v0_ref[rows, :]) + _dot(weights[1], v1_ref[rows, :])

    ahead = jnp.where(step + 1 < n_steps, step + 1, 0)
    for c in page_copies(ahead, 1 - slot):
        c.start()

    zero_c = jnp.zeros((SB_BLK, 1), F32)
    z_meta = _dot_nt(q2, kmb_ref[...])
    carries, weights = _sb_weights(scores(qi), biases, (zero_c, zero_c), diag_mask, neg_from)
    z_next = scores(jnp.maximum(qi - 1, 0))

    def body(n, state):
        carries, acc, z_cur, w_prev = state
        acc = weighted_values(acc, w_prev, qi - n)
        z_nxt = scores(jnp.maximum(qi - 2 - n, 0))
        carries, w_cur = _sb_weights(z_cur, biases, carries, None, neg_from)
        return carries, acc, z_nxt, w_cur

    carries, acc, _, weights = lax.fori_loop(
        0, qi, body, (carries, jnp.zeros((SB_BLK, 2 * SB_DH), F32), z_next, weights))
    acc = weighted_values(acc, weights, 0)

    for c in page_copies(step, slot):
        c.wait()
    part = step % steps_per_sample

    @pl.when(part == 0)
    def _():
        acc_ref[...] = jnp.zeros_like(acc_ref)
        carry_ref[...] = jnp.zeros_like(carry_ref)

    width = SB_HEADS * SB_DH
    hrow = lax.broadcasted_iota(jnp.int32, (SB_HEADS, width), 0)
    hlane = lax.broadcasted_iota(jnp.int32, (SB_HEADS, width), 1) // SB_DH
    own = hrow == hlane
    qd = jnp.broadcast_to(qd_ref[0] * (SB_DH ** -0.5), (SB_HEADS, width))
    qbd = jnp.where(own, qd, 0.0).astype(BF16)
    carry_d = carry_ref[...]
    acc_d = [acc_ref[h * SB_DH:(h + 1) * SB_DH, :] for h in range(SB_HEADS)]
    for c0 in range(0, step_pages, DEC_PAGES):
        idx = range(c0, c0 + DEC_PAGES)
        carry_d, acc_d = _decode_chunk(qbd, biasd_ref[...], [kbuf.at[slot, i] for i in idx],
                                       [vbuf.at[slot, i] for i in idx], carry_d, acc_d, neg_from_m)
    for h in range(SB_HEADS):
        acc_ref[h * SB_DH:(h + 1) * SB_DH, :] = acc_d[h]
    carry_ref[...] = carry_d

    @pl.when(part == steps_per_sample - 1)
    def _():
        od = jnp.sum(acc_ref[...].T, axis=0, keepdims=True)
        od_ref[0] = od * gated_ref[0]

    _, w_meta = _sb_weights(z_meta, biases, carries, meta_mask, neg_from_m)
    acc = acc + _dot(w_meta[0], vm0_ref[...]) + _dot(w_meta[1], vm1_ref[...])
    o_ref[0] = acc * gate_ref[0]

    @pl.when(step == n_steps - 1)
    def _():
        for c in page_copies(ahead, 1 - slot):
            c.wait()


def _merge_kernel(x_ref, a_ref, b_ref, w_ref, y_ref):
    half = a_ref.shape[-1]
    y = _dot(a_ref[...].astype(BF16), w_ref[0:half, :])
    y = y + _dot(b_ref[...].astype(BF16), w_ref[half:2 * half, :])
    y_ref[...] = x_ref[...] + y


def _merge(x2d, a, b, w_bf16, tm):
    rows, d = x2d.shape
    half = a.shape[-1]
    return pl.pallas_call(
        _merge_kernel,
        grid=(rows // tm,),
        in_specs=[
            pl.BlockSpec((tm, d), lambda i: (i, 0)),
            pl.BlockSpec((tm, half), lambda i: (i, 0)),
            pl.BlockSpec((tm, half), lambda i: (i, 0)),
            pl.BlockSpec(w_bf16.shape, lambda i: (0, 0)),
        ],
        out_specs=pl.BlockSpec((tm, d), lambda i: (i, 0)),
        out_shape=jax.ShapeDtypeStruct((rows, d), F32),
        compiler_params=pltpu.CompilerParams(
            dimension_semantics=("parallel",), vmem_limit_bytes=VMEM_LIMIT),
        name="merge",
    )(x2d, a, b, w_bf16)


def kernel(x_prompt, x_sample, cache_k, cache_v, state_hgrn, page_table, meta_tokens, norm_g,
           w_in, lb_logits, hg_norm_g, q_norm_g, k_norm_g, sb_bias, w_out):
    nb, seq, d = x_prompt.shape
    db = x_sample.shape[0]
    n_pages = page_table.shape[1]
    n_phys = cache_k.shape[1]
    assert w_in.shape[0] == 1 and w_in.shape[2] == N_SEG * SEG
    assert seq % SB_BLK == 0 and seq % HG_CHUNK == 0 and nb % HG_BATCH == 0
    assert db % STEP_SAMPLES == 0 and seq % PROJ_ROWS == 0

    w_in_b = w_in[0].astype(BF16)
    w_out_b = w_out[0].astype(BF16)
    g_row = norm_g[0][None, :]
    qg = jnp.tile(q_norm_g[0], SB_HEADS)[None, :]
    kg = jnp.tile(k_norm_g[0], SB_HEADS)[None, :]
    ng = hg_norm_g[0].reshape(1, HG_HEADS * HG_DIM)
    lane_head = jnp.arange(MXU_TILE) // SB_DH
    grp = (lane_head[:, None] == lane_head[None, :]).astype(BF16)
    bias = sb_bias[0].astype(F32)

    xs2d = x_sample.reshape(db, d)
    small = jnp.concatenate([xs2d, meta_tokens.astype(F32)], axis=0)
    n_small = small.shape[0]
    s_out = _project(small[None], n_small, g_row, w_in_b, lb_logits, qg, kg, grp)
    hq_s, lf_s, hv_s, hgate_s, sq_s, sk_s, sv_s, sgate_s = [a[0, :db] for a in s_out]
    _, lf_m, hv_m, _, _, sk_m, sv_m, _ = [a[0, db:] for a in s_out]
    hq, lf, hv, hgate, sq, sk_all, sv_all, sgate = _project(
        x_prompt, PROJ_ROWS, g_row, w_in_b, lb_logits, qg, kg, grp, lead_k=sk_m, lead_v=sv_m)

    pad_m = ((0, HG_CHUNK - N_META), (0, 0))
    st_meta = pl.pallas_call(
        _hgrn_meta_kernel,
        out_shape=jax.ShapeDtypeStruct((HG_HEADS, HG_DIM, HG_DIM), F32),
        name="hgrn_meta",
    )(jnp.pad(lf_m, pad_m), jnp.pad(hv_m, pad_m), _hgrn_sum_matrix(HG_CHUNK, False))

    rspec = pl.BlockSpec((STEP_SAMPLES, SEG), lambda i: (i, 0))
    sspec = pl.BlockSpec((STEP_SAMPLES, HG_HEADS, HG_DIM, HG_DIM), lambda i: (i, 0, 0, 0))
    s_sample, a_hg_s = pl.pallas_call(
        _hgrn_step_kernel,
        grid=(db // STEP_SAMPLES,),
        in_specs=[rspec, rspec, rspec, rspec, pl.BlockSpec((1, SEG), lambda i: (0, 0)), sspec],
        out_specs=[sspec, rspec],
        out_shape=[jax.ShapeDtypeStruct((db, HG_HEADS, HG_DIM, HG_DIM), F32),
                   jax.ShapeDtypeStruct((db, SEG), F32)],
        compiler_params=pltpu.CompilerParams(
            dimension_semantics=("parallel",), vmem_limit_bytes=VMEM_LIMIT),
        name="hgrn_step",
    )(hq_s, lf_s, hv_s, hgate_s, ng, state_hgrn[0])

    n_qb = seq // SB_BLK
    n_pairs = SB_HEADS // 2
    n_steps = nb * n_pairs * n_qb
    assert (db * n_pages) % n_steps == 0
    step_pages = db * n_pages // n_steps
    assert n_pages % step_pages == 0 and step_pages % DEC_PAGES == 0
    steps_per_sample = n_pages // step_pages
    ck = jnp.transpose(cache_k[0], (0, 2, 3, 1)).reshape(n_phys, SEG, PAGE)
    cv = jnp.transpose(cache_v[0], (0, 2, 3, 1)).reshape(n_phys, SEG, PAGE)
    bias_bc = jnp.broadcast_to(bias[:, None], (SB_HEADS, DEC_PAGES * PAGE))

    qspec = pl.BlockSpec((1, SB_BLK, 2 * SB_DH),
                         lambda b, p, i, *_: (b, _interleaved_block(i, n_qb), p))
    kvspec = pl.BlockSpec((1, N_META + seq, 2 * SB_DH), lambda b, p, i, *_: (b, 0, p))
    vec_spec = pl.BlockSpec(
        (1, 1, SEG), lambda b, p, i, *_: (((b * n_pairs + p) * n_qb + i) // steps_per_sample, 0, 0))
    hbm_spec = pl.BlockSpec(memory_space=pl.ANY)
    b_sb, b_sb_s = pl.pallas_call(
        _sb_kernel,
        grid_spec=pltpu.PrefetchScalarGridSpec(
            num_scalar_prefetch=2,
            grid=(nb, n_pairs, n_qb),
            in_specs=[qspec, kvspec, kvspec, qspec,
                      vec_spec, vec_spec,
                      pl.BlockSpec((SB_HEADS, DEC_PAGES * PAGE), lambda b, p, i, *_: (0, 0)),
                      hbm_spec, hbm_spec],
            out_specs=[qspec, vec_spec],
            scratch_shapes=[pltpu.VMEM((seq, 2 * SB_DH), BF16)] * 3
                           + [pltpu.VMEM((PAGE, 2 * SB_DH), BF16)] * 3
                           + [pltpu.VMEM((2, step_pages, SEG, PAGE), F32)] * 2
                           + [pltpu.SemaphoreType.DMA((2, 2)),
                              pltpu.VMEM((SEG, PAGE), F32), pltpu.VMEM((SB_HEADS, PAGE), F32)],
        ),
        out_shape=[jax.ShapeDtypeStruct((nb, seq, SEG), F32),
                   jax.ShapeDtypeStruct((db, 1, SEG), F32)],
        compiler_params=pltpu.CompilerParams(
            dimension_semantics=("arbitrary", "arbitrary", "arbitrary"),
            vmem_limit_bytes=SB_VMEM_LIMIT),
        name="sb_attention",
    )(bias, page_table,
      sq, sk_all, sv_all, sgate,
      sq_s.reshape(db, 1, SEG), sgate_s.reshape(db, 1, SEG), bias_bc, ck, cv)

    n_chunks = seq // HG_CHUNK
    sums = _hgrn_sum_matrix(HG_CHUNK, True)
    lvl = _hgrn_level_index(HG_CHUNK)
    cspec = pl.BlockSpec((HG_BATCH, HG_CHUNK, SEG), lambda b, c: (b, c, 0))
    xspec = pl.BlockSpec((HG_BATCH, HG_CHUNK, d), lambda b, c: (b, c, 0))
    y_prompt, s_prompt = pl.pallas_call(
        _hgrn_prompt_kernel,
        grid=(nb // HG_BATCH, n_chunks),
        in_specs=[cspec, cspec, cspec, cspec,
                  pl.BlockSpec((1, SEG), lambda b, c: (0, 0)),
                  pl.BlockSpec((HG_HEADS, HG_DIM, HG_DIM), lambda b, c: (0, 0, 0)),
                  pl.BlockSpec(sums.shape, lambda b, c: (0, 0)),
                  pl.BlockSpec(lvl.shape, lambda b, c: (0, 0)),
                  xspec, cspec,
                  pl.BlockSpec(w_out_b.shape, lambda b, c: (0, 0))],
        out_specs=[xspec,
                   pl.BlockSpec((HG_BATCH, HG_HEADS, HG_DIM, HG_DIM), lambda b, c: (b, 0, 0, 0))],
        out_shape=[jax.ShapeDtypeStruct((nb, seq, d), F32),
                   jax.ShapeDtypeStruct((nb, HG_HEADS, HG_DIM, HG_DIM), F32)],
        scratch_shapes=[pltpu.VMEM((HG_BATCH, HG_HEADS, HG_DIM, HG_DIM), F32)],
        compiler_params=pltpu.CompilerParams(
            dimension_semantics=("parallel", "arbitrary"), vmem_limit_bytes=VMEM_LIMIT),
        name="hgrn_prompt",
    )(hq, lf, hv, hgate, ng, st_meta, sums, lvl, x_prompt, b_sb, w_out_b)

    y_sample = _merge(xs2d, a_hg_s, b_sb_s.reshape(db, SEG), w_out_b, db)

    return (y_prompt,
            y_sample.reshape(db, 1, d),
            sk_all.reshape(1, nb, N_META + seq, SB_HEADS, SB_DH),
            sv_all.reshape(1, nb, N_META + seq, SB_HEADS, SB_DH),
            s_prompt[None],
            sk_s.reshape(1, db, 1, SB_HEADS, SB_DH),
            sv_s.reshape(1, db, 1, SB_HEADS, SB_DH),
            s_sample[None])
```
